```python
import math
import jax
import jax.numpy as jnp
from jax import lax
import numpy as np

D_MODEL = 4096
BATCH = 2
SEQ = 8192
DEPTH = 2

GRID_W = 64
CTX_LEN = 256
N_MIXERS = 2
N_LAYERS_A = (DEPTH + N_MIXERS - 1) // N_MIXERS
N_LAYERS_B = DEPTH // N_MIXERS
NA_HEAD_DIM = 128
NA_HEADS = D_MODEL // NA_HEAD_DIM
WIN_ROWS = 8
WIN_COLS = 16
DA_HEAD_DIM = 128
DA_HEADS = D_MODEL // (2 * DA_HEAD_DIM)
ROPE_BASE = 10000.0
Q_BLOCK = 128
N_EXPERTS = 16
EC_CAPACITY_FACTOR = 2
EXPERT_FF = D_MODEL // 4
N_MOD = 6
NORM_EPS = 1e-6
SUBLN_EPS = 1e-5
NEG_INF = -1e30

kernel_name = "hybrid_natten_diffattn_ecmoe_dit"


def rmsnorm(x, g, eps=NORM_EPS):
    x32 = x.astype(jnp.float32)
    y = x32 * lax.rsqrt(jnp.mean(x32 * x32, axis=-1, keepdims=True) + eps)
    return (y * g.astype(jnp.float32)).astype(x.dtype)


def modulate(h, shift, scale):
    return h * (1 + scale) + shift


def axial_rope_tables(n):
    t = jnp.arange(n, dtype=jnp.int32)
    row = (t // GRID_W).astype(jnp.float32)
    col = (t % GRID_W).astype(jnp.float32)
    pairs_per_axis = DA_HEAD_DIM // 4
    freq = ROPE_BASE ** (-jnp.arange(pairs_per_axis, dtype=jnp.float32) / pairs_per_axis)
    ang = jnp.concatenate([row[:, None] * freq, col[:, None] * freq], axis=-1)
    return jnp.cos(ang), jnp.sin(ang)


def apply_rope(x, cos, sin):
    xr = x.reshape(x.shape[:-1] + (x.shape[-1] // 2, 2))
    xe, xo = xr[..., 0], xr[..., 1]
    cs = cos[None, :, None, None, :].astype(x.dtype)
    sn = sin[None, :, None, None, :].astype(x.dtype)
    return jnp.stack([xe * cs - xo * sn, xe * sn + xo * cs], axis=-1).reshape(x.shape)


def softmax_attend(q, k, v):
    s = jnp.einsum('bqhd,bkhd->bhqk', q, k).astype(jnp.float32) * (q.shape[-1] ** -0.5)
    p = jax.nn.softmax(s, axis=-1).astype(v.dtype)
    return jnp.einsum('bhqk,bkhd->bqhd', p, v)


def natten_mixer(h_l, h_c, w_qkv, w_o, rpb, need_ctx):
    b, n, _ = h_l.shape
    rows = n // GRID_W
    kh = min(WIN_ROWS, rows)
    scale = NA_HEAD_DIM ** -0.5

    def proj(h):
        q, k, v = jnp.split(h @ w_qkv, 3, axis=-1)
        shp = h.shape[:2] + (NA_HEADS, NA_HEAD_DIM)
        return q.reshape(shp), k.reshape(shp), v.reshape(shp)

    q_l, k_l, v_l = proj(h_l)
    q_c, k_c, v_c = proj(h_c)
    grid = (b, rows, GRID_W, NA_HEADS, NA_HEAD_DIM)
    q_g, k_g, v_g = q_l.reshape(grid), k_l.reshape(grid), v_l.reshape(grid)

    qc = jnp.arange(GRID_W)
    col_start = jnp.clip(qc - WIN_COLS // 2, 0, GRID_W - WIN_COLS)
    col_mask = (qc[None, :] >= col_start[:, None]) & (qc[None, :] < col_start[:, None] + WIN_COLS)
    col_idx = jnp.clip(qc[None, :] - qc[:, None] + WIN_COLS - 1, 0, 2 * WIN_COLS - 2)

    def row_block(r):
        rs = jnp.clip(r - kh // 2, 0, rows - kh)
        q_r = lax.dynamic_index_in_dim(q_g, r, axis=1, keepdims=False)
        k_r = lax.dynamic_slice_in_dim(k_g, rs, kh, axis=1)
        v_r = lax.dynamic_slice_in_dim(v_g, rs, kh, axis=1)
        row_idx = rs + jnp.arange(kh) - r + (WIN_ROWS - 1)
        bias = rpb[:, row_idx][:, :, col_idx].transpose(0, 2, 1, 3).astype(jnp.float32)
        s_win = jnp.einsum('bqhd,bkwhd->bhqkw', q_r, k_r).astype(jnp.float32) * scale + bias
        s_win = jnp.where(col_mask[:, None, :], s_win, NEG_INF)
        s_ctx = jnp.einsum('bqhd,blhd->bhql', q_r, k_c).astype(jnp.float32) * scale
        s = jnp.concatenate([s_win.reshape(b, NA_HEADS, GRID_W, kh * GRID_W), s_ctx], axis=-1)
        p = jax.nn.softmax(s, axis=-1).astype(v_l.dtype)
        p_win = p[..., :kh * GRID_W].reshape(b, NA_HEADS, GRID_W, kh, GRID_W)
        p_ctx = p[..., kh * GRID_W:]
        return (jnp.einsum('bhqkw,bkwhd->bqhd', p_win, v_r)
                + jnp.einsum('bhql,blhd->bqhd', p_ctx, v_c))

    o_l = lax.map(row_block, jnp.arange(rows)).swapaxes(0, 1).reshape(b, n, D_MODEL)
    y_l = o_l @ w_o
    y_c = softmax_attend(q_c, k_c, v_c).reshape(h_c.shape) @ w_o if need_ctx else None
    return y_l, y_c


def diff_attend(q, k, v, lam):
    s = jnp.einsum('bqhcd,bkhcd->bhcqk', q, k).astype(jnp.float32) * (q.shape[-1] ** -0.5)
    p = jax.nn.softmax(s, axis=-1)
    a = (p[:, :, 0] - lam * p[:, :, 1]).astype(v.dtype)
    return jnp.einsum('bhqk,bkhe->bqhe', a, v)


def diff_mixer(h_l, h_c, w_qkv, w_o, lq1, lk1, lq2, lk2, subln_g, lambda_init, cos, sin, need_ctx):
    b, n, _ = h_l.shape

    def proj(h):
        q, k, v = jnp.split(h @ w_qkv, 3, axis=-1)
        m = h.shape[1]
        qk_shape = (b, m, DA_HEADS, 2, DA_HEAD_DIM)
        return q.reshape(qk_shape), k.reshape(qk_shape), v.reshape(b, m, DA_HEADS, 2 * DA_HEAD_DIM)

    q_l, k_l, v_l = proj(h_l)
    q_c, k_c, v_c = proj(h_c)
    q_l = apply_rope(q_l, cos, sin)
    k_l = apply_rope(k_l, cos, sin)
    f32 = jnp.float32
    lam = (jnp.exp(jnp.sum(lq1.astype(f32) * lk1.astype(f32)))
           - jnp.exp(jnp.sum(lq2.astype(f32) * lk2.astype(f32))) + lambda_init)

    def head_out(o):
        o = rmsnorm(o, subln_g, SUBLN_EPS) * (1.0 - lambda_init)
        return o.reshape(o.shape[0], o.shape[1], D_MODEL) @ w_o

    k_all = jnp.concatenate([k_l, k_c], axis=1)
    v_all = jnp.concatenate([v_l, v_c], axis=1)
    nb = n // Q_BLOCK
    q_blocks = q_l.reshape(b, nb, Q_BLOCK, DA_HEADS, 2, DA_HEAD_DIM).swapaxes(0, 1)
    o_l = lax.map(lambda qb: diff_attend(qb, k_all, v_all, lam), q_blocks)
    o_l = o_l.swapaxes(0, 1).reshape(b, n, DA_HEADS, 2 * DA_HEAD_DIM)
    y_l = head_out(o_l)
    y_c = head_out(diff_attend(q_c, k_c, v_c, lam)) if need_ctx else None
    return y_l, y_c


def ec_moe(h, w_router, w_gate, w_up, w_down):
    b, n, d = h.shape
    cap = EC_CAPACITY_FACTOR * n // N_EXPERTS
    aff = jax.nn.softmax(jnp.einsum('bnd,de->bne', h, w_router).astype(jnp.float32), axis=-1)
    gate, idx = lax.top_k(aff.swapaxes(1, 2), cap)
    xg = jax.vmap(lambda hb, ib: hb[ib])(h, idx)
    a = jax.nn.silu(jnp.einsum('becd,edf->becf', xg, w_gate)) * jnp.einsum('becd,edf->becf', xg, w_up)
    y = jnp.einsum('becf,efd->becd', a, w_down) * gate[..., None].astype(h.dtype)
    return jax.vmap(lambda yb, ib: jnp.zeros((n, d), h.dtype).at[ib.reshape(-1)].add(yb.reshape(-1, d)))(y, idx)


def lambda_init_fn(layer_idx):
    return 0.8 - 0.6 * math.exp(-0.3 * layer_idx)


def setup_inputs(seed: int = 0) -> dict:
    key = jax.random.key(seed)
    ks = jax.random.split(key, 24)
    D = D_MODEL

    def nrm(k, shape, scale):
        return jax.random.normal(k, shape, jnp.float32) * scale

    return {
        "x": nrm(ks[0], (BATCH, SEQ, D), 1.0),
        "c": nrm(ks[1], (BATCH, D), 1.0),
        "ctx": nrm(ks[2], (BATCH, CTX_LEN, D), 1.0),
        "c_ctx": nrm(ks[3], (D,), 1.0),
        "ada_w": nrm(ks[4], (DEPTH, D, N_MOD * D), 0.5 * D ** -0.5),
        "ada_b": nrm(ks[5], (DEPTH, N_MOD * D), 0.01),
        "norm1_g": 1.0 + nrm(ks[6], (DEPTH, D), 0.01),
        "norm2_g": 1.0 + nrm(ks[7], (DEPTH, D), 0.01),
        "na_w_qkv": nrm(ks[8], (N_LAYERS_A, D, 3 * D), D ** -0.5),
        "na_w_o": nrm(ks[9], (N_LAYERS_A, D, D), D ** -0.5),
        "na_rpb": nrm(ks[10], (N_LAYERS_A, NA_HEADS, 2 * WIN_ROWS - 1, 2 * WIN_COLS - 1), 0.02),
        "da_w_qkv": nrm(ks[11], (N_LAYERS_B, D, 3 * D), D ** -0.5),
        "da_w_o": nrm(ks[12], (N_LAYERS_B, D, D), D ** -0.5),
        "da_lambda_q1": nrm(ks[13], (N_LAYERS_B, DA_HEAD_DIM), 0.1),
        "da_lambda_k1": nrm(ks[14], (N_LAYERS_B, DA_HEAD_DIM), 0.1),
        "da_lambda_q2": nrm(ks[15], (N_LAYERS_B, DA_HEAD_DIM), 0.1),
        "da_lambda_k2": nrm(ks[16], (N_LAYERS_B, DA_HEAD_DIM), 0.1),
        "da_subln_g": 1.0 + nrm(ks[17], (N_LAYERS_B, 2 * DA_HEAD_DIM), 0.01),
        "moe_w_router": nrm(ks[18], (DEPTH, D, N_EXPERTS), D ** -0.5),
        "moe_w_gate": nrm(ks[19], (DEPTH, N_EXPERTS, D, EXPERT_FF), D ** -0.5),
        "moe_w_up": nrm(ks[20], (DEPTH, N_EXPERTS, D, EXPERT_FF), D ** -0.5),
        "moe_w_down": nrm(ks[21], (DEPTH, N_EXPERTS, EXPERT_FF, D), EXPERT_FF ** -0.5),
        "final_g": 1.0 + nrm(ks[22], (D,), 0.01),
    }


def reference(x, c, ctx, c_ctx, ada_w, ada_b, norm1_g, norm2_g, na_w_qkv, na_w_o, na_rpb,
              da_w_qkv, da_w_o, da_lambda_q1, da_lambda_k1, da_lambda_q2, da_lambda_k2, da_subln_g,
              moe_w_router, moe_w_gate, moe_w_up, moe_w_down, final_g):
    n = x.shape[1]
    cos, sin = axial_rope_tables(n)
    silu_c = jax.nn.silu(c)
    silu_cc = jax.nn.silu(c_ctx)
    for i in range(DEPTH):
        last = i == DEPTH - 1
        j = i // N_MIXERS
        mod_l = jnp.split((silu_c @ ada_w[i] + ada_b[i])[:, None, :], N_MOD, axis=-1)
        mod_c = jnp.split((silu_cc @ ada_w[i] + ada_b[i])[None, None, :], N_MOD, axis=-1)
        h_l = modulate(rmsnorm(x, norm1_g[i]), mod_l[0], mod_l[1])
        h_c = modulate(rmsnorm(ctx, norm1_g[i]), mod_c[0], mod_c[1])
        if i % N_MIXERS == 0:
            y_l, y_c = natten_mixer(h_l, h_c, na_w_qkv[j], na_w_o[j], na_rpb[j], not last)
        else:
            y_l, y_c = diff_mixer(h_l, h_c, da_w_qkv[j], da_w_o[j], da_lambda_q1[j], da_lambda_k1[j],
                                  da_lambda_q2[j], da_lambda_k2[j], da_subln_g[j], lambda_init_fn(i),
                                  cos, sin, not last)
        x = x + mod_l[2] * y_l
        x = x + mod_l[5] * ec_moe(modulate(rmsnorm(x, norm2_g[i]), mod_l[3], mod_l[4]),
                                  moe_w_router[i], moe_w_gate[i], moe_w_up[i], moe_w_down[i])
        if not last:
            ctx = ctx + mod_c[2] * y_c
            ctx = ctx + mod_c[5] * ec_moe(modulate(rmsnorm(ctx, norm2_g[i]), mod_c[3], mod_c[4]),
                                          moe_w_router[i], moe_w_gate[i], moe_w_up[i], moe_w_down[i])
    return rmsnorm(x, final_g)
```

```python
import functools
import math

import numpy as np
import jax
import jax.numpy as jnp
from jax import lax
from jax.experimental import pallas as pl
from jax.experimental.pallas import tpu as pltpu

F32 = jnp.float32
BF16 = jnp.bfloat16

GRID_W = 64
WIN_ROWS = 8
WIN_COLS = 16
HEAD_DIM = 128
N_MIXERS = 2
N_MOD = 6
ROPE_BASE = 10000.0
EC_CAPACITY_FACTOR = 2
NORM_EPS = 1e-6
SUBLN_EPS = 1e-5
NEG_INF = -1e30

V7X_VMEM_LIMIT_BYTES = 56 * 1024 * 1024
LANES = 128
COND_ROWS = 8
NA_ROW_BLOCK = 4
NA_KEY_ROWS = NA_ROW_BLOCK + WIN_ROWS - 1


def _params(*sem):
    return pltpu.CompilerParams(dimension_semantics=sem, vmem_limit_bytes=V7X_VMEM_LIMIT_BYTES)


def _lambda_init(layer_idx):
    return 0.8 - 0.6 * math.exp(-0.3 * layer_idx)


def _adaln_kernel(s_ref, w_ref, b_ref, o_ref):
    s = s_ref[...]
    s = s * (1.0 / (1.0 + jnp.exp(-s)))
    acc = jnp.dot(s.astype(BF16), w_ref[...].astype(BF16), preferred_element_type=F32)
    o_ref[...] = acc + b_ref[...]


def _adaln(cond, ada_w, ada_b3, layer):
    d = cond.shape[1]
    n6 = ada_w.shape[2]
    tn = min(512, n6)
    return pl.pallas_call(
        _adaln_kernel,
        out_shape=jax.ShapeDtypeStruct((COND_ROWS, n6), F32),
        grid=(n6 // tn,),
        in_specs=[
            pl.BlockSpec((COND_ROWS, d), lambda j: (0, 0)),
            pl.BlockSpec((None, d, tn), lambda j: (layer, 0, j)),
            pl.BlockSpec((None, 1, tn), lambda j: (layer, 0, j)),
        ],
        out_specs=pl.BlockSpec((COND_ROWS, tn), lambda j: (0, j)),
        compiler_params=_params("arbitrary"),
        name="adaln",
    )(cond, ada_w, ada_b3)


def _prep_kernel(x_ref, g_ref, *rest, eps, mod_row, modulate, with_router):
    rest = list(rest)
    if modulate:
        sh_ref, sc_ref = rest[:2]
        rest = rest[2:]
    if with_router:
        wr_ref, o_ref, aff_ref = rest
    else:
        (o_ref,) = rest
    x = x_ref[...]
    h = x * lax.rsqrt(jnp.mean(x * x, axis=-1, keepdims=True) + eps) * g_ref[...]
    if modulate:
        row = pl.program_id(0) if mod_row is None else mod_row
        h = h * (1.0 + sc_ref[pl.ds(row, 1), :]) + sh_ref[pl.ds(row, 1), :]
    o_ref[...] = h.astype(o_ref.dtype)
    if with_router:
        logits = lax.dot_general(wr_ref[...], h, (((1,), (1,)), ((), ())),
                                 precision=lax.Precision.HIGHEST, preferred_element_type=F32)
        e = jnp.exp(logits - jnp.max(logits, axis=0, keepdims=True))
        aff_ref[...] = e / jnp.sum(e, axis=0, keepdims=True)


def _prep(x, g, mod, *, shift_chunk=None, scale_chunk=None, mod_row=None, out_dtype=BF16,
          router_t=None, eps=NORM_EPS):
    b, n, d = x.shape
    tm = min(256, n)
    modulate = shift_chunk is not None
    with_router = router_t is not None
    in_specs = [pl.BlockSpec((None, tm, d), lambda bi, i: (bi, i, 0)),
                pl.BlockSpec((1, d), lambda bi, i: (0, 0))]
    args = [x, g.reshape(1, d)]
    if modulate:
        in_specs += [pl.BlockSpec((COND_ROWS, d), lambda bi, i: (0, shift_chunk)),
                     pl.BlockSpec((COND_ROWS, d), lambda bi, i: (0, scale_chunk))]
        args += [mod, mod]
    out_shape = [jax.ShapeDtypeStruct((b, n, d), out_dtype)]
    out_specs = [pl.BlockSpec((None, tm, d), lambda bi, i: (bi, i, 0))]
    if with_router:
        e = router_t.shape[0]
        in_specs.append(pl.BlockSpec((e, d), lambda bi, i: (0, 0)))
        args.append(router_t)
        out_shape.append(jax.ShapeDtypeStruct((b, e, n), F32))
        out_specs.append(pl.BlockSpec((None, e, tm), lambda bi, i: (bi, 0, i)))
    out = pl.pallas_call(
        functools.partial(_prep_kernel, eps=eps, mod_row=mod_row, modulate=modulate,
                          with_router=with_router),
        out_shape=out_shape,
        grid=(b, n // tm),
        in_specs=in_specs,
        out_specs=out_specs,
        compiler_params=_params("arbitrary", "arbitrary"),
        name="prep",
    )(*args)
    return out if with_router else out[0]


def _mm_kernel(a_ref, w_ref, *rest, mode, n_rope_tiles, tiles_per_batch, mod_row):
    acc = jnp.dot(a_ref[...], w_ref[...], preferred_element_type=F32)
    if mode == "plain":
        (o_ref,) = rest
        o_ref[...] = acc.astype(o_ref.dtype)
    elif mode == "rope":
        cos_ref, sin_ref, o_ref = rest
        j = pl.program_id(1)

        @pl.when(j < n_rope_tiles)
        def _():
            cos = cos_ref[...]
            sin = sin_ref[...]
            for c in range(acc.shape[1] // LANES):
                xc = acc[:, c * LANES:(c + 1) * LANES]
                rot = xc * cos + pltpu.roll(xc, LANES // 2, 1) * sin
                o_ref[:, c * LANES:(c + 1) * LANES] = rot.astype(o_ref.dtype)

        @pl.when(j >= n_rope_tiles)
        def _():
            o_ref[...] = acc.astype(o_ref.dtype)
    else:
        x_ref, gate_ref, o_ref = rest
        row = pl.program_id(0) // tiles_per_batch if mod_row is None else mod_row
        o_ref[...] = x_ref[...] + gate_ref[pl.ds(row, 1), :] * acc


def _matmul(a, w, *, mode="plain", out_dtype=BF16, rope=None, n_rope_cols=0, resid=None, mod=None,
            gate_chunk=None, rows_per_batch=None, mod_row=None):
    m, k = a.shape
    n = w.shape[1]
    tm = min(1024, m)
    tn = min(512, n)
    if rows_per_batch is not None:
        tm = min(tm, rows_per_batch)
    in_specs = [pl.BlockSpec((tm, k), lambda i, j: (i, 0)),
                pl.BlockSpec((k, tn), lambda i, j: (0, j))]
    args = [a, w]
    tiles_per_batch = 1
    if mode == "rope":
        cos, sin = rope
        nblk = cos.shape[0] // tm
        in_specs += [pl.BlockSpec((tm, LANES), lambda i, j: (i % nblk, 0)),
                     pl.BlockSpec((tm, LANES), lambda i, j: (i % nblk, 0))]
        args += [cos, sin]
    elif mode == "resid":
        tiles_per_batch = rows_per_batch // tm
        gate_off = gate_chunk * (n // tn)
        in_specs += [pl.BlockSpec((tm, tn), lambda i, j: (i, j)),
                     pl.BlockSpec((COND_ROWS, tn), lambda i, j: (0, gate_off + j))]
        args += [resid, mod]
    return pl.pallas_call(
        functools.partial(_mm_kernel, mode=mode, n_rope_tiles=n_rope_cols // tn,
                          tiles_per_batch=tiles_per_batch, mod_row=mod_row),
        out_shape=jax.ShapeDtypeStruct((m, n), out_dtype),
        grid=(m // tm, n // tn),
        in_specs=in_specs,
        out_specs=pl.BlockSpec((tm, tn), lambda i, j: (i, j)),
        compiler_params=_params("arbitrary", "arbitrary"),
        name="matmul_" + mode,
    )(*args)


def _natten_kernel(q_ref, k_ref, v_ref, kc_ref, vc_ref, bias_ref, o_ref, *, rows, scale):
    r = pl.program_id(2)
    ws = jnp.clip(r * NA_ROW_BLOCK - WIN_ROWS // 2, 0, rows - NA_KEY_ROWS)
    start = pl.multiple_of(ws * GRID_W, GRID_W)
    nk = NA_KEY_ROWS * GRID_W
    q = q_ref[...]
    k = k_ref[pl.ds(start, nk), :]
    v = v_ref[pl.ds(start, nk), :]
    nt = (((1,), (1,)), ((), ()))
    s = lax.dot_general(q, k, nt, preferred_element_type=F32) * scale + bias_ref[...]
    sc = lax.dot_general(q, kc_ref[...], nt, preferred_element_type=F32) * scale
    m = jnp.maximum(jnp.max(s, axis=-1, keepdims=True), jnp.max(sc, axis=-1, keepdims=True))
    p = jnp.exp(s - m)
    pc = jnp.exp(sc - m)
    l = jnp.sum(p, axis=-1, keepdims=True) + jnp.sum(pc, axis=-1, keepdims=True)
    o = (jnp.dot(p.astype(BF16), v, preferred_element_type=F32)
         + jnp.dot(pc.astype(BF16), vc_ref[...], preferred_element_type=F32))
    o_ref[...] = (o / l).astype(o_ref.dtype)


def _natten_bias(rpb, rows):
    nblk = rows // NA_ROW_BLOCK
    kh = WIN_ROWS
    qc = np.arange(GRID_W)
    col_start = np.clip(qc - WIN_COLS // 2, 0, GRID_W - WIN_COLS)
    col_mask = (qc[None, :] >= col_start[:, None]) & (qc[None, :] < col_start[:, None] + WIN_COLS)
    col_idx = np.clip(qc[None, :] - qc[:, None] + WIN_COLS - 1, 0, 2 * WIN_COLS - 2)
    row_idx = np.zeros((3, NA_ROW_BLOCK, NA_KEY_ROWS), np.int32)
    valid = np.zeros((3, NA_ROW_BLOCK, NA_KEY_ROWS), bool)
    for cls, blk in enumerate((0, 1, nblk - 1)):
        r0 = blk * NA_ROW_BLOCK
        ws = int(np.clip(r0 - WIN_ROWS // 2, 0, rows - NA_KEY_ROWS))
        for a in range(NA_ROW_BLOCK):
            r = r0 + a
            rs = int(np.clip(r - kh // 2, 0, rows - kh))
            for kr in range(NA_KEY_ROWS):
                key_row = ws + kr
                if rs <= key_row < rs + kh:
                    valid[cls, a, kr] = True
                    row_idx[cls, a, kr] = key_row - r + (WIN_ROWS - 1)
    for blk in range(1, nblk - 1):
        r0 = blk * NA_ROW_BLOCK
        ws = int(np.clip(r0 - WIN_ROWS // 2, 0, rows - NA_KEY_ROWS))
        for a in range(NA_ROW_BLOCK):
            rs = int(np.clip(r0 + a - kh // 2, 0, rows - kh))
            assert rs - ws == a and r0 - ws == WIN_ROWS // 2
    t = rpb[:, row_idx]
    t = t[..., col_idx]
    mask = valid[None, :, :, :, None, None] & col_mask[None, None, None, None]
    t = jnp.where(mask, t.astype(F32), NEG_INF)
    t = t.transpose(1, 0, 2, 4, 3, 5)
    h = rpb.shape[0]
    return t.reshape(3, h, NA_ROW_BLOCK * GRID_W, NA_KEY_ROWS * GRID_W)


def _natten(qkv_l, qkv_c, bias):
    b, n, d3 = qkv_l.shape
    d = d3 // 3
    nh = d // HEAD_DIM
    ctx = qkv_c.shape[1]
    rows = n // GRID_W
    nblk = rows // NA_ROW_BLOCK
    tq = NA_ROW_BLOCK * GRID_W
    nk = NA_KEY_ROWS * GRID_W

    def bias_map(bi, h, r):
        cls = jnp.where(r == 0, 0, jnp.where(r == nblk - 1, 2, 1))
        return (cls, h, 0, 0)

    return pl.pallas_call(
        functools.partial(_natten_kernel, rows=rows, scale=HEAD_DIM ** -0.5),
        out_shape=jax.ShapeDtypeStruct((b, n, d), BF16),
        grid=(b, nh, nblk),
        in_specs=[
            pl.BlockSpec((None, tq, HEAD_DIM), lambda bi, h, r: (bi, r, h)),
            pl.BlockSpec((None, n, HEAD_DIM), lambda bi, h, r: (bi, 0, nh + h)),
            pl.BlockSpec((None, n, HEAD_DIM), lambda bi, h, r: (bi, 0, 2 * nh + h)),
            pl.BlockSpec((None, ctx, HEAD_DIM), lambda bi, h, r: (bi, 0, nh + h)),
            pl.BlockSpec((None, ctx, HEAD_DIM), lambda bi, h, r: (bi, 0, 2 * nh + h)),
            pl.BlockSpec((None, None, tq, nk), bias_map),
        ],
        out_specs=pl.BlockSpec((None, tq, HEAD_DIM), lambda bi, h, r: (bi, r, h)),
        compiler_params=_params("arbitrary", "arbitrary", "arbitrary"),
        name="natten",
    )(qkv_l, qkv_l, qkv_l, qkv_c, qkv_c, bias)


def _ctx_attn_kernel(q_ref, k_ref, v_ref, o_ref, *, scale):
    s = lax.dot_general(q_ref[...], k_ref[...], (((1,), (1,)), ((), ())),
                        preferred_element_type=F32) * scale
    p = jnp.exp(s - jnp.max(s, axis=-1, keepdims=True))
    l = jnp.sum(p, axis=-1, keepdims=True)
    o = jnp.dot(p.astype(BF16), v_ref[...], preferred_element_type=F32)
    o_ref[...] = (o / l).astype(o_ref.dtype)


def _ctx_attn(qkv_c):
    b, ctx, d3 = qkv_c.shape
    d = d3 // 3
    nh = d // HEAD_DIM
    return pl.pallas_call(
        functools.partial(_ctx_attn_kernel, scale=HEAD_DIM ** -0.5),
        out_shape=jax.ShapeDtypeStruct((b, ctx, d), BF16),
        grid=(b, nh),
        in_specs=[
            pl.BlockSpec((None, ctx, HEAD_DIM), lambda bi, h: (bi, 0, h)),
            pl.BlockSpec((None, ctx, HEAD_DIM), lambda bi, h: (bi, 0, nh + h)),
            pl.BlockSpec((None, ctx, HEAD_DIM), lambda bi, h: (bi, 0, 2 * nh + h)),
        ],
        out_specs=pl.BlockSpec((None, ctx, HEAD_DIM), lambda bi, h: (bi, 0, h)),
        compiler_params=_params("arbitrary", "arbitrary"),
        name="ctx_attn",
    )(qkv_c, qkv_c, qkv_c)


def _diff_kernel(q_ref, k_ref, v_ref, kc_ref, vc_ref, lq1_ref, lk1_ref, lq2_ref, lk2_ref, g_ref,
                 o_ref, m_sc, l_sc, acc_sc, *, tk, lambda_init, scale):
    n = k_ref.shape[0]
    nt = (((1,), (1,)), ((), ()))
    m_sc[...] = jnp.full(m_sc.shape, NEG_INF, F32)
    l_sc[...] = jnp.zeros(l_sc.shape, F32)
    acc_sc[...] = jnp.zeros(acc_sc.shape, F32)

    def step(k_blk, v_blk):
        for c in range(2):
            q = q_ref[:, c * HEAD_DIM:(c + 1) * HEAD_DIM]
            s = lax.dot_general(q, k_blk[:, c * HEAD_DIM:(c + 1) * HEAD_DIM], nt,
                                preferred_element_type=F32) * scale
            m_prev = m_sc[c]
            m_new = jnp.maximum(m_prev, jnp.max(s, axis=-1, keepdims=True))
            alpha = jnp.exp(m_prev - m_new)
            p = jnp.exp(s - m_new)
            l_sc[c] = alpha * l_sc[c] + jnp.sum(p, axis=-1, keepdims=True)
            acc_sc[c] = alpha * acc_sc[c] + jnp.dot(p.astype(BF16), v_blk,
                                                    preferred_element_type=F32)
            m_sc[c] = m_new

    def body(j, carry):
        st = pl.multiple_of(j * tk, tk)
        step(k_ref[pl.ds(st, tk), :], v_ref[pl.ds(st, tk), :])
        return carry

    lax.fori_loop(0, n // tk, body, 0)
    step(kc_ref[...], vc_ref[...])

    lam = (jnp.exp(jnp.sum(lq1_ref[...] * lk1_ref[...], axis=-1, keepdims=True))
           - jnp.exp(jnp.sum(lq2_ref[...] * lk2_ref[...], axis=-1, keepdims=True)) + lambda_init)
    o = acc_sc[0] / l_sc[0] - lam * (acc_sc[1] / l_sc[1])
    o = o * lax.rsqrt(jnp.mean(o * o, axis=-1, keepdims=True) + SUBLN_EPS) * g_ref[...]
    o_ref[...] = (o * (1.0 - lambda_init)).astype(o_ref.dtype)


def _diff_attn(qkv_l, qkv_c, lq1, lk1, lq2, lk2, subln_g, lambda_init):
    b, n, d3 = qkv_l.shape
    d = d3 // 3
    hw = 2 * HEAD_DIM
    nh = d // hw
    ctx = qkv_c.shape[1]
    tq = min(512, n)
    tk = min(512, n)
    vec = lambda a: a.reshape(1, -1).astype(F32)
    small = pl.BlockSpec((1, HEAD_DIM), lambda bi, h, i: (0, 0))
    return pl.pallas_call(
        functools.partial(_diff_kernel, tk=tk, lambda_init=lambda_init, scale=HEAD_DIM ** -0.5),
        out_shape=jax.ShapeDtypeStruct((b, n, d), BF16),
        grid=(b, nh, n // tq),
        in_specs=[
            pl.BlockSpec((None, tq, hw), lambda bi, h, i: (bi, i, h)),
            pl.BlockSpec((None, n, hw), lambda bi, h, i: (bi, 0, nh + h)),
            pl.BlockSpec((None, n, hw), lambda bi, h, i: (bi, 0, 2 * nh + h)),
            pl.BlockSpec((None, ctx, hw), lambda bi, h, i: (bi, 0, nh + h)),
            pl.BlockSpec((None, ctx, hw), lambda bi, h, i: (bi, 0, 2 * nh + h)),
            small, small, small, small,
            pl.BlockSpec((1, hw), lambda bi, h, i: (0, 0)),
        ],
        out_specs=pl.BlockSpec((None, tq, hw), lambda bi, h, i: (bi, i, h)),
        scratch_shapes=[pltpu.VMEM((2, tq, 1), F32), pltpu.VMEM((2, tq, 1), F32),
                        pltpu.VMEM((2, tq, hw), F32)],
        compiler_params=_params("arbitrary", "arbitrary", "arbitrary"),
        name="diff_attn",
    )(qkv_l, qkv_l, qkv_l, qkv_c, qkv_c, vec(lq1), vec(lk1), vec(lq2), vec(lk2), vec(subln_g))


def _ffn_kernel(x_ref, gate_ref, wg_ref, wu_ref, wd_ref, o_ref):
    f = pl.program_id(3)

    @pl.when(f == 0)
    def _():
        o_ref[...] = jnp.zeros(o_ref.shape, F32)

    x = x_ref[...]
    g = jnp.dot(x, wg_ref[...], preferred_element_type=F32)
    u = jnp.dot(x, wu_ref[...], preferred_element_type=F32)
    a = g * (1.0 / (1.0 + jnp.exp(-g))) * u
    o_ref[...] += jnp.dot(a.astype(BF16), wd_ref[...], preferred_element_type=F32)

    @pl.when(f == pl.num_programs(3) - 1)
    def _():
        o_ref[...] = o_ref[...] * gate_ref[...]


def _ffn(xg, gate, wg, wu, wd):
    b, e, c, d = xg.shape
    ff = wg.shape[2]
    tc = min(512, c)
    tf = min(256, ff)
    return pl.pallas_call(
        _ffn_kernel,
        out_shape=jax.ShapeDtypeStruct((b, e, c, d), F32),
        grid=(e, b, c // tc, ff // tf),
        in_specs=[
            pl.BlockSpec((None, None, tc, d), lambda ei, bi, ci, f: (bi, ei, ci, 0)),
            pl.BlockSpec((None, None, tc, 1), lambda ei, bi, ci, f: (bi, ei, ci, 0)),
            pl.BlockSpec((None, d, tf), lambda ei, bi, ci, f: (ei, 0, f)),
            pl.BlockSpec((None, d, tf), lambda ei, bi, ci, f: (ei, 0, f)),
            pl.BlockSpec((None, tf, d), lambda ei, bi, ci, f: (ei, f, 0)),
        ],
        out_specs=pl.BlockSpec((None, None, tc, d), lambda ei, bi, ci, f: (bi, ei, ci, 0)),
        compiler_params=_params("arbitrary", "arbitrary", "arbitrary", "arbitrary"),
        name="expert_ffn",
    )(xg, gate, wg, wu, wd)


def _ec_moe(h, aff, wg, wu, wd):
    b, n, d = h.shape
    e = aff.shape[1]
    cap = EC_CAPACITY_FACTOR * n // e
    gate, idx = lax.top_k(aff, cap)
    xg = jax.vmap(lambda hb, ib: hb[ib])(h, idx)
    y = _ffn(xg, gate[..., None], wg, wu, wd)
    return jax.vmap(lambda yb, ib: jnp.zeros((n, d), F32).at[ib.reshape(-1)].add(yb.reshape(-1, d)))(y, idx)


def _rope_tables(n):
    t = np.arange(n)
    row = (t // GRID_W).astype(np.float32)
    col = (t % GRID_W).astype(np.float32)
    pairs = HEAD_DIM // 4
    freq = jnp.asarray(ROPE_BASE, F32) ** (-jnp.arange(pairs, dtype=F32) / pairs)
    ang = jnp.concatenate([jnp.asarray(row)[:, None] * freq, jnp.asarray(col)[:, None] * freq], axis=-1)
    cos, sin = jnp.cos(ang), jnp.sin(ang)
    return jnp.concatenate([cos, cos], axis=-1), jnp.concatenate([-sin, sin], axis=-1)


def _halves_perm(d):
    blk = np.concatenate([np.arange(0, HEAD_DIM, 2), np.arange(1, HEAD_DIM, 2)])
    qk = (np.arange(0, 2 * d, HEAD_DIM)[:, None] + blk[None, :]).reshape(-1)
    return np.concatenate([qk, np.arange(2 * d, 3 * d)])


def kernel(x, c, ctx, c_ctx, ada_w, ada_b, norm1_g, norm2_g, na_w_qkv, na_w_o, na_rpb, da_w_qkv, da_w_o, da_lambda_q1, da_lambda_k1, da_lambda_q2, da_lambda_k2, da_subln_g, moe_w_router, moe_w_gate, moe_w_up, moe_w_down, final_g):
    b, n, d = x.shape
    n_ctx = ctx.shape[1]
    depth = ada_w.shape[0]
    rows = n // GRID_W
    assert n % GRID_W == 0 and rows % NA_ROW_BLOCK == 0 and rows // NA_ROW_BLOCK >= 3
    assert b + 1 <= COND_ROWS

    cond = jnp.zeros((COND_ROWS, d), F32).at[:b].set(c).at[b].set(c_ctx)
    ada_b3 = ada_b.reshape(depth, 1, N_MOD * d)
    cos, sin = _rope_tables(n)
    perm = _halves_perm(d)

    for i in range(depth):
        last = i == depth - 1
        j = i // N_MIXERS
        mod = _adaln(cond, ada_w, ada_b3, i)
        h_l = _prep(x, norm1_g[i], mod, shift_chunk=0, scale_chunk=1)
        h_c = _prep(ctx, norm1_g[i], mod, shift_chunk=0, scale_chunk=1, mod_row=b)
        if i % N_MIXERS == 0:
            w_qkv = na_w_qkv[j].astype(BF16)
            w_o = na_w_o[j].astype(BF16)
            qkv_l = _matmul(h_l.reshape(b * n, d), w_qkv).reshape(b, n, 3 * d)
            qkv_c = _matmul(h_c.reshape(b * n_ctx, d), w_qkv).reshape(b, n_ctx, 3 * d)
            o_l = _natten(qkv_l, qkv_c, _natten_bias(na_rpb[j], rows))
            o_c = None if last else _ctx_attn(qkv_c)
        else:
            w_qkv = da_w_qkv[j][:, perm].astype(BF16)
            w_o = da_w_o[j].astype(BF16)
            qkv_l = _matmul(h_l.reshape(b * n, d), w_qkv, mode="rope", rope=(cos, sin),
                            n_rope_cols=2 * d, rows_per_batch=n).reshape(b, n, 3 * d)
            qkv_c = _matmul(h_c.reshape(b * n_ctx, d), w_qkv).reshape(b, n_ctx, 3 * d)
            o_l = _diff_attn(qkv_l, qkv_c, da_lambda_q1[j], da_lambda_k1[j], da_lambda_q2[j],
                             da_lambda_k2[j], da_subln_g[j], _lambda_init(i))
            assert last, "context update after a differential layer is not needed at this depth"
            o_c = None
        x = _matmul(o_l.reshape(b * n, d), w_o, mode="resid", out_dtype=F32, resid=x.reshape(b * n, d),
                    mod=mod, gate_chunk=2, rows_per_batch=n).reshape(b, n, d)
        router_t = moe_w_router[i].T
        wg = moe_w_gate[i].astype(BF16)
        wu = moe_w_up[i].astype(BF16)
        wd = moe_w_down[i].astype(BF16)
        h2, aff = _prep(x, norm2_g[i], mod, shift_chunk=3, scale_chunk=4, router_t=router_t)
        x = x + mod[:b, None, 5 * d:6 * d] * _ec_moe(h2, aff, wg, wu, wd)
        if not last:
            ctx = _matmul(o_c.reshape(b * n_ctx, d), w_o, mode="resid", out_dtype=F32,
                          resid=ctx.reshape(b * n_ctx, d), mod=mod, gate_chunk=2,
                          rows_per_batch=n_ctx, mod_row=b).reshape(b, n_ctx, d)
            h2c, affc = _prep(ctx, norm2_g[i], mod, shift_chunk=3, scale_chunk=4, mod_row=b,
                              router_t=router_t)
            ctx = ctx + mod[b, 5 * d:6 * d] * _ec_moe(h2c, affc, wg, wu, wd)
    return _prep(x, final_g, None, out_dtype=F32)
```

```python
import functools
import math

import numpy as np
import jax
import jax.numpy as jnp
from jax import lax
from jax.experimental import pallas as pl
from jax.experimental.pallas import tpu as pltpu

F32 = jnp.float32
BF16 = jnp.bfloat16

GRID_W = 64
WIN_ROWS = 8
WIN_COLS = 16
HEAD_DIM = 128
N_MIXERS = 2
N_MOD = 6
ROPE_BASE = 10000.0
EC_CAPACITY_FACTOR = 2
NORM_EPS = 1e-6
SUBLN_EPS = 1e-5
NEG_INF = -1e30
LOG2E = math.log2(math.e)
Q_SCALE = HEAD_DIM ** -0.5 * LOG2E

V7X_VMEM_LIMIT_BYTES = 56 * 1024 * 1024
LANES = 128
COND_ROWS = 8
NA_ROW_BLOCK = 4
NA_KEY_ROWS = NA_ROW_BLOCK + WIN_ROWS - 1
NA_HEADS_PER_STEP = 2
DA_Q_TILE = 1024
DA_K_TILE = 512
DA_UNROLL = 4


def _params(*sem):
    return pltpu.CompilerParams(dimension_semantics=sem, vmem_limit_bytes=V7X_VMEM_LIMIT_BYTES)


def _lambda_init(layer_idx):
    return 0.8 - 0.6 * math.exp(-0.3 * layer_idx)


def _adaln_kernel(s_ref, w_ref, b_ref, o_ref):
    s = s_ref[...]
    s = s * (1.0 / (1.0 + jnp.exp(-s)))
    acc = jnp.dot(s.astype(BF16), w_ref[...].astype(BF16), preferred_element_type=F32)
    o_ref[...] = acc + b_ref[...]


def _adaln(cond, ada_w, ada_b3, layer):
    d = cond.shape[1]
    n6 = ada_w.shape[2]
    tn = min(512, n6)
    return pl.pallas_call(
        _adaln_kernel,
        out_shape=jax.ShapeDtypeStruct((COND_ROWS, n6), F32),
        grid=(n6 // tn,),
        in_specs=[
            pl.BlockSpec((COND_ROWS, d), lambda j: (0, 0)),
            pl.BlockSpec((None, d, tn), lambda j: (layer, 0, j)),
            pl.BlockSpec((None, 1, tn), lambda j: (layer, 0, j)),
        ],
        out_specs=pl.BlockSpec((COND_ROWS, tn), lambda j: (0, j)),
        compiler_params=_params("arbitrary"),
        name="adaln",
    )(cond, ada_w, ada_b3)


def _prep_kernel(x_ref, g_ref, *rest, eps, mod_row, modulate, with_router):
    rest = list(rest)
    if modulate:
        sh_ref, sc_ref = rest[:2]
        rest = rest[2:]
    if with_router:
        wr_ref, o_ref, aff_ref = rest
    else:
        (o_ref,) = rest
    x = x_ref[...]
    h = x * lax.rsqrt(jnp.mean(x * x, axis=-1, keepdims=True) + eps) * g_ref[...]
    if modulate:
        row = pl.program_id(0) if mod_row is None else mod_row
        h = h * (1.0 + sc_ref[pl.ds(row, 1), :]) + sh_ref[pl.ds(row, 1), :]
    o_ref[...] = h.astype(o_ref.dtype)
    if with_router:
        logits = lax.dot_general(wr_ref[...], h, (((1,), (1,)), ((), ())),
                                 precision=lax.Precision.HIGHEST, preferred_element_type=F32)
        e = jnp.exp(logits - jnp.max(logits, axis=0, keepdims=True))
        aff_ref[...] = e / jnp.sum(e, axis=0, keepdims=True)


def _prep(x, g, mod, *, shift_chunk=None, scale_chunk=None, mod_row=None, out_dtype=BF16,
          router_t=None, eps=NORM_EPS):
    b, n, d = x.shape
    tm = min(256, n)
    modulate = shift_chunk is not None
    with_router = router_t is not None
    in_specs = [pl.BlockSpec((None, tm, d), lambda bi, i: (bi, i, 0)),
                pl.BlockSpec((1, d), lambda bi, i: (0, 0))]
    args = [x, g.reshape(1, d)]
    if modulate:
        in_specs += [pl.BlockSpec((COND_ROWS, d), lambda bi, i: (0, shift_chunk)),
                     pl.BlockSpec((COND_ROWS, d), lambda bi, i: (0, scale_chunk))]
        args += [mod, mod]
    out_shape = [jax.ShapeDtypeStruct((b, n, d), out_dtype)]
    out_specs = [pl.BlockSpec((None, tm, d), lambda bi, i: (bi, i, 0))]
    if with_router:
        e = router_t.shape[0]
        in_specs.append(pl.BlockSpec((e, d), lambda bi, i: (0, 0)))
        args.append(router_t)
        out_shape.append(jax.ShapeDtypeStruct((b, e, n), F32))
        out_specs.append(pl.BlockSpec((None, e, tm), lambda bi, i: (bi, 0, i)))
    out = pl.pallas_call(
        functools.partial(_prep_kernel, eps=eps, mod_row=mod_row, modulate=modulate,
                          with_router=with_router),
        out_shape=out_shape,
        grid=(b, n // tm),
        in_specs=in_specs,
        out_specs=out_specs,
        compiler_params=_params("arbitrary", "arbitrary"),
        name="prep",
    )(*args)
    return out if with_router else out[0]


def _mm_kernel(a_ref, w_ref, *rest, mode, rope, nq_tiles, tiles_per_batch, mod_row):
    o_ref, wb_ref = rest[-2:]
    j = pl.program_id(0)
    i = pl.program_id(1)

    @pl.when(i == 0)
    def _():
        wb_ref[...] = w_ref[...].astype(BF16)

    acc = jnp.dot(a_ref[...], wb_ref[...], preferred_element_type=F32)
    if mode == "resid":
        x_ref, gate_ref = rest[:2]
        row = i // tiles_per_batch if mod_row is None else mod_row
        o_ref[...] = x_ref[...] + gate_ref[pl.ds(row, 1), :] * acc
        return

    def emit(rotate, scale):
        if rotate:
            cos = rest[0][...]
            sin = rest[1][...]
            even = lax.broadcasted_iota(jnp.int32, cos.shape, 1) % 2 == 0
        for c in range(acc.shape[1] // LANES):
            xc = acc[:, c * LANES:(c + 1) * LANES]
            if rotate:
                partner = jnp.where(even, pltpu.roll(xc, LANES - 1, 1), pltpu.roll(xc, 1, 1))
                xc = xc * cos + partner * sin
            if scale is not None:
                xc = xc * scale
            o_ref[:, c * LANES:(c + 1) * LANES] = xc.astype(o_ref.dtype)

    @pl.when(j < nq_tiles)
    def _():
        emit(rope, Q_SCALE)

    @pl.when((j >= nq_tiles) & (j < 2 * nq_tiles))
    def _():
        emit(rope, None)

    @pl.when(j >= 2 * nq_tiles)
    def _():
        emit(False, None)


def _matmul(a, w, layer, *, mode, out_dtype=BF16, rope=None, resid=None, mod=None, gate_chunk=None,
            rows_per_batch=None, mod_row=None):
    m, k = a.shape
    n = w.shape[2]
    tm = min(1024, m)
    tn = min(512, n)
    if rows_per_batch is not None:
        tm = min(tm, rows_per_batch)
    in_specs = [pl.BlockSpec((tm, k), lambda j, i: (i, 0)),
                pl.BlockSpec((None, k, tn), lambda j, i: (layer, 0, j))]
    args = [a, w]
    tiles_per_batch = 1
    nq_tiles = 0
    if mode == "qkv":
        nq_tiles = n // 3 // tn
        if rope is not None:
            cos, sin = rope
            nblk = cos.shape[0] // tm
            in_specs += [pl.BlockSpec((tm, LANES), lambda j, i: (i % nblk, 0)),
                         pl.BlockSpec((tm, LANES), lambda j, i: (i % nblk, 0))]
            args += [cos, sin]
    else:
        tiles_per_batch = rows_per_batch // tm
        gate_off = gate_chunk * (n // tn)
        in_specs += [pl.BlockSpec((tm, tn), lambda j, i: (i, j)),
                     pl.BlockSpec((COND_ROWS, tn), lambda j, i: (0, gate_off + j))]
        args += [resid, mod]
    return pl.pallas_call(
        functools.partial(_mm_kernel, mode=mode, rope=rope is not None, nq_tiles=nq_tiles,
                          tiles_per_batch=tiles_per_batch, mod_row=mod_row),
        out_shape=jax.ShapeDtypeStruct((m, n), out_dtype),
        grid=(n // tn, m // tm),
        in_specs=in_specs,
        out_specs=pl.BlockSpec((tm, tn), lambda j, i: (i, j)),
        scratch_shapes=[pltpu.VMEM((k, tn), BF16)],
        compiler_params=_params("arbitrary", "arbitrary"),
        name="matmul_" + mode,
    )(*args)


def _natten_kernel(q_ref, k_ref, v_ref, kc_ref, vc_ref, bias_ref, o_ref, *, rows):
    r = pl.program_id(2)
    ws = jnp.clip(r * NA_ROW_BLOCK - WIN_ROWS // 2, 0, rows - NA_KEY_ROWS)
    start = pl.multiple_of(ws * GRID_W, GRID_W)
    nk = NA_KEY_ROWS * GRID_W
    nt = (((1,), (1,)), ((), ()))
    for h in range(NA_HEADS_PER_STEP):
        sl = slice(h * HEAD_DIM, (h + 1) * HEAD_DIM)
        q = q_ref[:, sl]
        k = k_ref[pl.ds(start, nk), sl]
        v = v_ref[pl.ds(start, nk), sl]
        s = lax.dot_general(q, k, nt, preferred_element_type=F32) + bias_ref[h]
        sc = lax.dot_general(q, kc_ref[:, sl], nt, preferred_element_type=F32)
        m = jnp.maximum(jnp.max(s, axis=-1, keepdims=True), jnp.max(sc, axis=-1, keepdims=True))
        p = jnp.exp2(s - m)
        pc = jnp.exp2(sc - m)
        l = jnp.sum(p, axis=-1, keepdims=True) + jnp.sum(pc, axis=-1, keepdims=True)
        o = (jnp.dot(p.astype(BF16), v, preferred_element_type=F32)
             + jnp.dot(pc.astype(BF16), vc_ref[:, sl], preferred_element_type=F32))
        o_ref[:, sl] = (o / l).astype(o_ref.dtype)


def _natten_bias(rpb, rows):
    nblk = rows // NA_ROW_BLOCK
    kh = WIN_ROWS
    qc = np.arange(GRID_W, dtype=np.int32)
    col_start = np.clip(qc - WIN_COLS // 2, 0, GRID_W - WIN_COLS)
    col_mask = (qc[None, :] >= col_start[:, None]) & (qc[None, :] < col_start[:, None] + WIN_COLS)
    col_idx = np.clip(qc[None, :] - qc[:, None] + WIN_COLS - 1, 0, 2 * WIN_COLS - 2).astype(np.int32)
    row_idx = np.zeros((3, NA_ROW_BLOCK, NA_KEY_ROWS), np.int32)
    valid = np.zeros((3, NA_ROW_BLOCK, NA_KEY_ROWS), bool)
    for cls, blk in enumerate((0, 1, nblk - 1)):
        r0 = blk * NA_ROW_BLOCK
        ws = int(np.clip(r0 - WIN_ROWS // 2, 0, rows - NA_KEY_ROWS))
        for a in range(NA_ROW_BLOCK):
            r = r0 + a
            rs = int(np.clip(r - kh // 2, 0, rows - kh))
            for kr in range(NA_KEY_ROWS):
                key_row = ws + kr
                if rs <= key_row < rs + kh:
                    valid[cls, a, kr] = True
                    row_idx[cls, a, kr] = key_row - r + (WIN_ROWS - 1)
    for blk in range(1, nblk - 1):
        r0 = blk * NA_ROW_BLOCK
        ws = int(np.clip(r0 - WIN_ROWS // 2, 0, rows - NA_KEY_ROWS))
        for a in range(NA_ROW_BLOCK):
            rs = int(np.clip(r0 + a - kh // 2, 0, rows - kh))
            assert rs - ws == a and r0 - ws == WIN_ROWS // 2
    t = rpb[:, row_idx]
    t = t[..., col_idx]
    mask = valid[None, :, :, :, None, None] & col_mask[None, None, None, None]
    t = jnp.where(mask, t.astype(F32) * LOG2E, NEG_INF)
    t = t.transpose(1, 0, 2, 4, 3, 5)
    h = rpb.shape[0]
    return t.reshape(3, h // NA_HEADS_PER_STEP, NA_HEADS_PER_STEP, NA_ROW_BLOCK * GRID_W,
                     NA_KEY_ROWS * GRID_W)


def _natten(qkv_l, qkv_c, bias):
    b, n, d3 = qkv_l.shape
    d = d3 // 3
    hw = NA_HEADS_PER_STEP * HEAD_DIM
    ng = d // hw
    ctx = qkv_c.shape[1]
    rows = n // GRID_W
    nblk = rows // NA_ROW_BLOCK
    tq = NA_ROW_BLOCK * GRID_W
    nk = NA_KEY_ROWS * GRID_W

    def bias_map(bi, h, r):
        cls = jnp.where(r == 0, 0, jnp.where(r == nblk - 1, 2, 1))
        return (cls, h, 0, 0, 0)

    return pl.pallas_call(
        functools.partial(_natten_kernel, rows=rows),
        out_shape=jax.ShapeDtypeStruct((b, n, d), BF16),
        grid=(b, ng, nblk),
        in_specs=[
            pl.BlockSpec((None, tq, hw), lambda bi, h, r: (bi, r, h)),
            pl.BlockSpec((None, n, hw), lambda bi, h, r: (bi, 0, ng + h)),
            pl.BlockSpec((None, n, hw), lambda bi, h, r: (bi, 0, 2 * ng + h)),
            pl.BlockSpec((None, ctx, hw), lambda bi, h, r: (bi, 0, ng + h)),
            pl.BlockSpec((None, ctx, hw), lambda bi, h, r: (bi, 0, 2 * ng + h)),
            pl.BlockSpec((None, None, NA_HEADS_PER_STEP, tq, nk), bias_map),
        ],
        out_specs=pl.BlockSpec((None, tq, hw), lambda bi, h, r: (bi, r, h)),
        compiler_params=_params("arbitrary", "arbitrary", "arbitrary"),
        name="natten",
    )(qkv_l, qkv_l, qkv_l, qkv_c, qkv_c, bias)


def _ctx_attn_kernel(q_ref, k_ref, v_ref, o_ref):
    s = lax.dot_general(q_ref[...], k_ref[...], (((1,), (1,)), ((), ())), preferred_element_type=F32)
    p = jnp.exp2(s - jnp.max(s, axis=-1, keepdims=True))
    l = jnp.sum(p, axis=-1, keepdims=True)
    o = jnp.dot(p.astype(BF16), v_ref[...], preferred_element_type=F32)
    o_ref[...] = (o / l).astype(o_ref.dtype)


def _ctx_attn(qkv_c):
    b, ctx, d3 = qkv_c.shape
    d = d3 // 3
    nh = d // HEAD_DIM
    return pl.pallas_call(
        _ctx_attn_kernel,
        out_shape=jax.ShapeDtypeStruct((b, ctx, d), BF16),
        grid=(b, nh),
        in_specs=[
            pl.BlockSpec((None, ctx, HEAD_DIM), lambda bi, h: (bi, 0, h)),
            pl.BlockSpec((None, ctx, HEAD_DIM), lambda bi, h: (bi, 0, nh + h)),
            pl.BlockSpec((None, ctx, HEAD_DIM), lambda bi, h: (bi, 0, 2 * nh + h)),
        ],
        out_specs=pl.BlockSpec((None, ctx, HEAD_DIM), lambda bi, h: (bi, 0, h)),
        compiler_params=_params("arbitrary", "arbitrary"),
        name="ctx_attn",
    )(qkv_c, qkv_c, qkv_c)


def _diff_kernel(q_ref, k_ref, v_ref, kc_ref, vc_ref, lq1_ref, lk1_ref, lq2_ref, lk2_ref, g_ref,
                 o_ref, m1, m2, l1, l2, a1, a2, *, tk, unroll, lambda_init):
    n = k_ref.shape[0]
    nt = (((1,), (1,)), ((), ()))
    ms, ls, accs = (m1, m2), (l1, l2), (a1, a2)
    for c in range(2):
        ms[c][...] = jnp.full(ms[c].shape, NEG_INF, F32)
        ls[c][...] = jnp.zeros(ls[c].shape, F32)
        accs[c][...] = jnp.zeros(accs[c].shape, F32)

    def step(k_blk, v_blk):
        for c in range(2):
            q = q_ref[:, c * HEAD_DIM:(c + 1) * HEAD_DIM]
            s = lax.dot_general(q, k_blk[:, c * HEAD_DIM:(c + 1) * HEAD_DIM], nt,
                                preferred_element_type=F32)
            m_prev = ms[c][...]
            m_new = jnp.maximum(m_prev, jnp.max(s, axis=-1, keepdims=True))
            alpha = jnp.exp2(m_prev - m_new)
            p = jnp.exp2(s - pltpu.repeat(m_new, s.shape[1] // LANES, 1))
            ls[c][...] = alpha * ls[c][...] + jnp.sum(p, axis=-1, keepdims=True)
            pv = jnp.dot(p.astype(BF16), v_blk, preferred_element_type=F32)
            accs[c][...] = pltpu.repeat(alpha, pv.shape[1] // LANES, 1) * accs[c][...] + pv
            ms[c][...] = m_new

    def body(j, carry):
        st = pl.multiple_of(j * tk, tk)
        step(k_ref[pl.ds(st, tk), :], v_ref[pl.ds(st, tk), :])
        return carry

    lax.fori_loop(0, n // tk, body, 0, unroll=unroll)
    step(kc_ref[...], vc_ref[...])

    lam = (jnp.exp(jnp.sum(lq1_ref[...] * lk1_ref[...], axis=-1, keepdims=True))
           - jnp.exp(jnp.sum(lq2_ref[...] * lk2_ref[...], axis=-1, keepdims=True)) + lambda_init)
    rep = a1.shape[1] // LANES
    o = a1[...] / pltpu.repeat(l1[...], rep, 1) - lam * (a2[...] / pltpu.repeat(l2[...], rep, 1))
    o = o * lax.rsqrt(jnp.mean(o * o, axis=-1, keepdims=True) + SUBLN_EPS) * g_ref[...]
    o_ref[...] = (o * (1.0 - lambda_init)).astype(o_ref.dtype)


def _diff_attn(qkv_l, qkv_c, lq1, lk1, lq2, lk2, subln_g, lambda_init):
    b, n, d3 = qkv_l.shape
    d = d3 // 3
    hw = 2 * HEAD_DIM
    nh = d // hw
    ctx = qkv_c.shape[1]
    tq = min(DA_Q_TILE, n)
    tk = min(DA_K_TILE, n)
    vec = lambda a: a.reshape(1, -1).astype(F32)
    small = pl.BlockSpec((1, HEAD_DIM), lambda bi, h, i: (0, 0))
    return pl.pallas_call(
        functools.partial(_diff_kernel, tk=tk, unroll=min(DA_UNROLL, n // tk), lambda_init=lambda_init),
        out_shape=jax.ShapeDtypeStruct((b, n, d), BF16),
        grid=(b, nh, n // tq),
        in_specs=[
            pl.BlockSpec((None, tq, hw), lambda bi, h, i: (bi, i, h)),
            pl.BlockSpec((None, n, hw), lambda bi, h, i: (bi, 0, nh + h)),
            pl.BlockSpec((None, n, hw), lambda bi, h, i: (bi, 0, 2 * nh + h)),
            pl.BlockSpec((None, ctx, hw), lambda bi, h, i: (bi, 0, nh + h)),
            pl.BlockSpec((None, ctx, hw), lambda bi, h, i: (bi, 0, 2 * nh + h)),
            small, small, small, small,
            pl.BlockSpec((1, hw), lambda bi, h, i: (0, 0)),
        ],
        out_specs=pl.BlockSpec((None, tq, hw), lambda bi, h, i: (bi, i, h)),
        scratch_shapes=[pltpu.VMEM((tq, LANES), F32)] * 4 + [pltpu.VMEM((tq, hw), F32)] * 2,
        compiler_params=_params("arbitrary", "arbitrary", "arbitrary"),
        name="diff_attn",
    )(qkv_l, qkv_l, qkv_l, qkv_c, qkv_c, vec(lq1), vec(lk1), vec(lq2), vec(lk2), vec(subln_g))


def _ffn_kernel(x_ref, gate_ref, mod_ref, wg_ref, wu_ref, wd_ref, o_ref, a_ref, *, nf, tf,
                tiles_per_batch, mod_row):
    s = pl.program_id(2)

    @pl.when(s < nf)
    def _():
        x = x_ref[...]
        g = jnp.dot(x, wg_ref[...].astype(BF16), preferred_element_type=F32)
        u = jnp.dot(x, wu_ref[...].astype(BF16), preferred_element_type=F32)
        a = (g * (1.0 / (1.0 + jnp.exp(-g))) * u).astype(BF16)
        for f in range(nf):
            @pl.when(s == f)
            def _():
                a_ref[:, f * tf:(f + 1) * tf] = a

    @pl.when(s >= nf)
    def _():
        y = jnp.dot(a_ref[...], wd_ref[...].astype(BF16), preferred_element_type=F32)
        row = pl.program_id(1) // tiles_per_batch if mod_row is None else mod_row
        o_ref[...] = y * gate_ref[...] * mod_ref[pl.ds(row, 1), :]


def _ffn(xg, gate, mod, w_gate, w_up, w_down, layer, *, rows_per_batch, mod_row=None):
    e, r, d = xg.shape
    ff = w_gate.shape[3]
    tc = min(1024, r if mod_row is not None else rows_per_batch)
    tf = min(256, ff)
    tn = min(512, d)
    nf = ff // tf
    nd = d // tn
    if mod_row is None:
        assert rows_per_batch % tc == 0
    tiles_per_batch = max(rows_per_batch // tc, 1)
    mod_off = (N_MOD - 1) * nd
    return pl.pallas_call(
        functools.partial(_ffn_kernel, nf=nf, tf=tf, tiles_per_batch=tiles_per_batch, mod_row=mod_row),
        out_shape=jax.ShapeDtypeStruct((e, r, d), F32),
        grid=(e, r // tc, nf + nd),
        in_specs=[
            pl.BlockSpec((None, tc, d), lambda ei, mi, s: (ei, mi, 0), pipeline_mode=pl.Buffered(1)),
            pl.BlockSpec((None, tc, 1), lambda ei, mi, s: (ei, mi, 0)),
            pl.BlockSpec((COND_ROWS, tn), lambda ei, mi, s: (0, mod_off + jnp.maximum(s - nf, 0))),
            pl.BlockSpec((None, None, d, tf), lambda ei, mi, s: (layer, ei, 0, jnp.minimum(s, nf - 1))),
            pl.BlockSpec((None, None, d, tf), lambda ei, mi, s: (layer, ei, 0, jnp.minimum(s, nf - 1))),
            pl.BlockSpec((None, None, ff, tn), lambda ei, mi, s: (layer, ei, 0, jnp.maximum(s - nf, 0))),
        ],
        out_specs=pl.BlockSpec((None, tc, tn), lambda ei, mi, s: (ei, mi, jnp.maximum(s - nf, 0))),
        scratch_shapes=[pltpu.VMEM((tc, ff), BF16)],
        compiler_params=_params("arbitrary", "arbitrary", "arbitrary"),
        name="expert_ffn",
    )(xg, gate, mod, w_gate, w_up, w_down)


def _ec_moe(x, h, aff, mod, w_gate, w_up, w_down, layer, mod_row=None):
    b, n, d = h.shape
    e = aff.shape[1]
    cap = EC_CAPACITY_FACTOR * n // e
    gate, idx = lax.top_k(aff, cap)
    idx_t = idx.transpose(1, 0, 2)
    bidx = jnp.arange(b, dtype=jnp.int32)[None, :, None]
    xg = h[bidx, idx_t].reshape(e, b * cap, d)
    gate_t = gate.transpose(1, 0, 2).reshape(e, b * cap, 1)
    y = _ffn(xg, gate_t, mod, w_gate, w_up, w_down, layer, rows_per_batch=cap, mod_row=mod_row)
    return x.at[bidx, idx_t].add(y.reshape(e, b, cap, d))


def _rope_tables(n):
    t = np.arange(n, dtype=np.int32)
    row = (t // GRID_W).astype(np.float32)
    col = (t % GRID_W).astype(np.float32)
    pairs = HEAD_DIM // 4
    freq = jnp.asarray(ROPE_BASE, F32) ** (-jnp.arange(pairs, dtype=F32) / pairs)
    ang = jnp.concatenate([jnp.asarray(row)[:, None] * freq, jnp.asarray(col)[:, None] * freq], axis=-1)
    cos, sin = jnp.cos(ang), jnp.sin(ang)
    return (jnp.stack([cos, cos], axis=-1).reshape(n, HEAD_DIM),
            jnp.stack([-sin, sin], axis=-1).reshape(n, HEAD_DIM))


def kernel(x, c, ctx, c_ctx, ada_w, ada_b, norm1_g, norm2_g, na_w_qkv, na_w_o, na_rpb, da_w_qkv, da_w_o, da_lambda_q1, da_lambda_k1, da_lambda_q2, da_lambda_k2, da_subln_g, moe_w_router, moe_w_gate, moe_w_up, moe_w_down, final_g):
    b, n, d = x.shape
    n_ctx = ctx.shape[1]
    depth = ada_w.shape[0]
    rows = n // GRID_W
    assert n % GRID_W == 0 and rows % NA_ROW_BLOCK == 0 and rows // NA_ROW_BLOCK >= 3
    assert b + 1 <= COND_ROWS

    cond = jnp.zeros((COND_ROWS, d), F32).at[:b].set(c).at[b].set(c_ctx)
    ada_b3 = ada_b.reshape(depth, 1, N_MOD * d)
    rope = _rope_tables(n)

    for i in range(depth):
        last = i == depth - 1
        j = i // N_MIXERS
        mod = _adaln(cond, ada_w, ada_b3, i)
        h_l = _prep(x, norm1_g[i], mod, shift_chunk=0, scale_chunk=1).reshape(b * n, d)
        h_c = _prep(ctx, norm1_g[i], mod, shift_chunk=0, scale_chunk=1, mod_row=b).reshape(b * n_ctx, d)
        if i % N_MIXERS == 0:
            w_o = na_w_o
            qkv_l = _matmul(h_l, na_w_qkv, j, mode="qkv").reshape(b, n, 3 * d)
            qkv_c = _matmul(h_c, na_w_qkv, j, mode="qkv").reshape(b, n_ctx, 3 * d)
            o_l = _natten(qkv_l, qkv_c, _natten_bias(na_rpb[j], rows))
            o_c = None if last else _ctx_attn(qkv_c)
        else:
            w_o = da_w_o
            qkv_l = _matmul(h_l, da_w_qkv, j, mode="qkv", rope=rope, rows_per_batch=n).reshape(b, n, 3 * d)
            qkv_c = _matmul(h_c, da_w_qkv, j, mode="qkv").reshape(b, n_ctx, 3 * d)
            o_l = _diff_attn(qkv_l, qkv_c, da_lambda_q1[j], da_lambda_k1[j], da_lambda_q2[j],
                             da_lambda_k2[j], da_subln_g[j], _lambda_init(i))
            assert last, "context update after a differential layer is not needed at this depth"
            o_c = None
        x = _matmul(o_l.reshape(b * n, d), w_o, j, mode="resid", out_dtype=F32, resid=x.reshape(b * n, d),
                    mod=mod, gate_chunk=2, rows_per_batch=n).reshape(b, n, d)
        router_t = moe_w_router[i].T
        h2, aff = _prep(x, norm2_g[i], mod, shift_chunk=3, scale_chunk=4, router_t=router_t)
        x = _ec_moe(x, h2, aff, mod, moe_w_gate, moe_w_up, moe_w_down, i)
        if not last:
            ctx = _matmul(o_c.reshape(b * n_ctx, d), w_o, j, mode="resid", out_dtype=F32,
                          resid=ctx.reshape(b * n_ctx, d), mod=mod, gate_chunk=2,
                          rows_per_batch=n_ctx, mod_row=b).reshape(b, n_ctx, d)
            h2c, affc = _prep(ctx, norm2_g[i], mod, shift_chunk=3, scale_chunk=4, mod_row=b,
                              router_t=router_t)
            ctx = _ec_moe(ctx, h2c, affc, mod, moe_w_gate, moe_w_up, moe_w_down, i, mod_row=b)
    return _prep(x, final_g, None, out_dtype=F32)
```

```python
import functools
import math

import numpy as np
import jax
import jax.numpy as jnp
from jax import lax
from jax.experimental import pallas as pl
from jax.experimental.pallas import tpu as pltpu

F32 = jnp.float32
BF16 = jnp.bfloat16

GRID_W = 64
WIN_ROWS = 8
WIN_COLS = 16
HEAD_DIM = 128
N_MIXERS = 2
N_MOD = 6
ROPE_BASE = 10000.0
EC_CAPACITY_FACTOR = 2
NORM_EPS = 1e-6
SUBLN_EPS = 1e-5
NEG_INF = -1e30
LOG2E = math.log2(math.e)
Q_SCALE = HEAD_DIM ** -0.5 * LOG2E

V7X_VMEM_LIMIT_BYTES = 56 * 1024 * 1024
LANES = 128
COND_ROWS = 8
NA_ROW_BLOCK = 4
NA_KEY_ROWS = NA_ROW_BLOCK + WIN_ROWS - 1
NA_HEADS_PER_STEP = 2
DA_Q_TILE = 1024
DA_K_TILE = 512
DA_UNROLL = 4


def _params(*sem):
    return pltpu.CompilerParams(dimension_semantics=sem, vmem_limit_bytes=V7X_VMEM_LIMIT_BYTES)


def _lambda_init(layer_idx):
    return 0.8 - 0.6 * math.exp(-0.3 * layer_idx)


def _adaln_kernel(s_ref, w_ref, b_ref, o_ref):
    s = s_ref[...]
    s = s * (1.0 / (1.0 + jnp.exp(-s)))
    acc = jnp.dot(s.astype(BF16), w_ref[...].astype(BF16), preferred_element_type=F32)
    o_ref[...] = acc + b_ref[...]


def _adaln(cond, ada_w, ada_b3, layer):
    d = cond.shape[1]
    n6 = ada_w.shape[2]
    tn = min(512, n6)
    return pl.pallas_call(
        _adaln_kernel,
        out_shape=jax.ShapeDtypeStruct((COND_ROWS, n6), F32),
        grid=(n6 // tn,),
        in_specs=[
            pl.BlockSpec((COND_ROWS, d), lambda j: (0, 0)),
            pl.BlockSpec((None, d, tn), lambda j: (layer, 0, j)),
            pl.BlockSpec((None, 1, tn), lambda j: (layer, 0, j)),
        ],
        out_specs=pl.BlockSpec((COND_ROWS, tn), lambda j: (0, j)),
        compiler_params=_params("arbitrary"),
        name="adaln",
    )(cond, ada_w, ada_b3)


def _prep_kernel(x_ref, g_ref, *rest, eps, mod_row, modulate, with_router):
    rest = list(rest)
    if modulate:
        sh_ref, sc_ref = rest[:2]
        rest = rest[2:]
    if with_router:
        wr_ref, o_ref, aff_ref = rest
    else:
        (o_ref,) = rest
    x = x_ref[...]
    h = x * lax.rsqrt(jnp.mean(x * x, axis=-1, keepdims=True) + eps) * g_ref[...]
    if modulate:
        row = pl.program_id(0) if mod_row is None else mod_row
        h = h * (1.0 + sc_ref[pl.ds(row, 1), :]) + sh_ref[pl.ds(row, 1), :]
    o_ref[...] = h.astype(o_ref.dtype)
    if with_router:
        logits = lax.dot_general(wr_ref[...], h, (((1,), (1,)), ((), ())),
                                 precision=lax.Precision.HIGHEST, preferred_element_type=F32)
        e = jnp.exp(logits - jnp.max(logits, axis=0, keepdims=True))
        aff_ref[...] = e / jnp.sum(e, axis=0, keepdims=True)


def _prep(x, g, mod, *, shift_chunk=None, scale_chunk=None, mod_row=None, out_dtype=BF16,
          router_t=None, eps=NORM_EPS):
    b, n, d = x.shape
    tm = min(256, n)
    modulate = shift_chunk is not None
    with_router = router_t is not None
    in_specs = [pl.BlockSpec((None, tm, d), lambda bi, i: (bi, i, 0)),
                pl.BlockSpec((1, d), lambda bi, i: (0, 0))]
    args = [x, g.reshape(1, d)]
    if modulate:
        in_specs += [pl.BlockSpec((COND_ROWS, d), lambda bi, i: (0, shift_chunk)),
                     pl.BlockSpec((COND_ROWS, d), lambda bi, i: (0, scale_chunk))]
        args += [mod, mod]
    out_shape = [jax.ShapeDtypeStruct((b, n, d), out_dtype)]
    out_specs = [pl.BlockSpec((None, tm, d), lambda bi, i: (bi, i, 0))]
    if with_router:
        e = router_t.shape[0]
        in_specs.append(pl.BlockSpec((e, d), lambda bi, i: (0, 0)))
        args.append(router_t)
        out_shape.append(jax.ShapeDtypeStruct((b, e, n), F32))
        out_specs.append(pl.BlockSpec((None, e, tm), lambda bi, i: (bi, 0, i)))
    out = pl.pallas_call(
        functools.partial(_prep_kernel, eps=eps, mod_row=mod_row, modulate=modulate,
                          with_router=with_router),
        out_shape=out_shape,
        grid=(b, n // tm),
        in_specs=in_specs,
        out_specs=out_specs,
        compiler_params=_params("arbitrary", "arbitrary"),
        name="prep",
    )(*args)
    return out if with_router else out[0]


def _mm_kernel(a_ref, w_ref, *rest, mode, rope, nq_tiles, tiles_per_batch, mod_row):
    o_ref, wb_ref = rest[-2:]
    j = pl.program_id(0)
    i = pl.program_id(1)

    @pl.when(i == 0)
    def _():
        wb_ref[...] = w_ref[...].astype(BF16)

    acc = jnp.dot(a_ref[...], wb_ref[...], preferred_element_type=F32)
    if mode == "resid":
        x_ref, gate_ref = rest[:2]
        row = i // tiles_per_batch if mod_row is None else mod_row
        o_ref[...] = x_ref[...] + gate_ref[pl.ds(row, 1), :] * acc
        return

    def emit(rotate, scale):
        if rotate:
            cos = rest[0][...]
            sin = rest[1][...]
            even = lax.broadcasted_iota(jnp.int32, cos.shape, 1) % 2 == 0
        for c in range(acc.shape[1] // LANES):
            xc = acc[:, c * LANES:(c + 1) * LANES]
            if rotate:
                partner = jnp.where(even, pltpu.roll(xc, LANES - 1, 1), pltpu.roll(xc, 1, 1))
                xc = xc * cos + partner * sin
            if scale is not None:
                xc = xc * scale
            o_ref[:, c * LANES:(c + 1) * LANES] = xc.astype(o_ref.dtype)

    @pl.when(j < nq_tiles)
    def _():
        emit(rope, Q_SCALE)

    @pl.when((j >= nq_tiles) & (j < 2 * nq_tiles))
    def _():
        emit(rope, None)

    @pl.when(j >= 2 * nq_tiles)
    def _():
        emit(False, None)


def _matmul(a, w, layer, *, mode, out_dtype=BF16, rope=None, resid=None, mod=None, gate_chunk=None,
            rows_per_batch=None, mod_row=None):
    m, k = a.shape
    n = w.shape[2]
    tm = min(1024, m)
    tn = min(512, n)
    if rows_per_batch is not None:
        tm = min(tm, rows_per_batch)
    in_specs = [pl.BlockSpec((tm, k), lambda j, i: (i, 0)),
                pl.BlockSpec((None, k, tn), lambda j, i: (layer, 0, j))]
    args = [a, w]
    tiles_per_batch = 1
    nq_tiles = 0
    if mode == "qkv":
        nq_tiles = n // 3 // tn
        if rope is not None:
            cos, sin = rope
            nblk = cos.shape[0] // tm
            in_specs += [pl.BlockSpec((tm, LANES), lambda j, i: (i % nblk, 0)),
                         pl.BlockSpec((tm, LANES), lambda j, i: (i % nblk, 0))]
            args += [cos, sin]
    else:
        tiles_per_batch = rows_per_batch // tm
        gate_off = gate_chunk * (n // tn)
        in_specs += [pl.BlockSpec((tm, tn), lambda j, i: (i, j)),
                     pl.BlockSpec((COND_ROWS, tn), lambda j, i: (0, gate_off + j))]
        args += [resid, mod]
    return pl.pallas_call(
        functools.partial(_mm_kernel, mode=mode, rope=rope is not None, nq_tiles=nq_tiles,
                          tiles_per_batch=tiles_per_batch, mod_row=mod_row),
        out_shape=jax.ShapeDtypeStruct((m, n), out_dtype),
        grid=(n // tn, m // tm),
        in_specs=in_specs,
        out_specs=pl.BlockSpec((tm, tn), lambda j, i: (i, j)),
        scratch_shapes=[pltpu.VMEM((k, tn), BF16)],
        compiler_params=_params("arbitrary", "arbitrary"),
        name="matmul_" + mode,
    )(*args)


def _natten_kernel(q_ref, k_ref, v_ref, kc_ref, vc_ref, bias_ref, o_ref, *, rows):
    r = pl.program_id(2)
    ws = jnp.clip(r * NA_ROW_BLOCK - WIN_ROWS // 2, 0, rows - NA_KEY_ROWS)
    start = pl.multiple_of(ws * GRID_W, GRID_W)
    nk = NA_KEY_ROWS * GRID_W
    nt = (((1,), (1,)), ((), ()))
    for h in range(NA_HEADS_PER_STEP):
        sl = slice(h * HEAD_DIM, (h + 1) * HEAD_DIM)
        q = q_ref[:, sl]
        k = k_ref[pl.ds(start, nk), sl]
        v = v_ref[pl.ds(start, nk), sl]
        s = lax.dot_general(q, k, nt, preferred_element_type=F32) + bias_ref[h]
        sc = lax.dot_general(q, kc_ref[:, sl], nt, preferred_element_type=F32)
        m = jnp.maximum(jnp.max(s, axis=-1, keepdims=True), jnp.max(sc, axis=-1, keepdims=True))
        p = jnp.exp2(s - m)
        pc = jnp.exp2(sc - m)
        l = jnp.sum(p, axis=-1, keepdims=True) + jnp.sum(pc, axis=-1, keepdims=True)
        o = (jnp.dot(p.astype(BF16), v, preferred_element_type=F32)
             + jnp.dot(pc.astype(BF16), vc_ref[:, sl], preferred_element_type=F32))
        o_ref[:, sl] = (o / l).astype(o_ref.dtype)


def _natten_bias(rpb, rows):
    nblk = rows // NA_ROW_BLOCK
    kh = WIN_ROWS
    qc = np.arange(GRID_W, dtype=np.int32)
    col_start = np.clip(qc - WIN_COLS // 2, 0, GRID_W - WIN_COLS)
    col_mask = (qc[None, :] >= col_start[:, None]) & (qc[None, :] < col_start[:, None] + WIN_COLS)
    col_idx = np.clip(qc[None, :] - qc[:, None] + WIN_COLS - 1, 0, 2 * WIN_COLS - 2).astype(np.int32)
    row_idx = np.zeros((3, NA_ROW_BLOCK, NA_KEY_ROWS), np.int32)
    valid = np.zeros((3, NA_ROW_BLOCK, NA_KEY_ROWS), bool)
    for cls, blk in enumerate((0, 1, nblk - 1)):
        r0 = blk * NA_ROW_BLOCK
        ws = int(np.clip(r0 - WIN_ROWS // 2, 0, rows - NA_KEY_ROWS))
        for a in range(NA_ROW_BLOCK):
            r = r0 + a
            rs = int(np.clip(r - kh // 2, 0, rows - kh))
            for kr in range(NA_KEY_ROWS):
                key_row = ws + kr
                if rs <= key_row < rs + kh:
                    valid[cls, a, kr] = True
                    row_idx[cls, a, kr] = key_row - r + (WIN_ROWS - 1)
    for blk in range(1, nblk - 1):
        r0 = blk * NA_ROW_BLOCK
        ws = int(np.clip(r0 - WIN_ROWS // 2, 0, rows - NA_KEY_ROWS))
        for a in range(NA_ROW_BLOCK):
            rs = int(np.clip(r0 + a - kh // 2, 0, rows - kh))
            assert rs - ws == a and r0 - ws == WIN_ROWS // 2
    t = rpb[:, row_idx]
    t = t[..., col_idx]
    mask = valid[None, :, :, :, None, None] & col_mask[None, None, None, None]
    t = jnp.where(mask, t.astype(F32) * LOG2E, NEG_INF)
    t = t.transpose(1, 0, 2, 4, 3, 5)
    h = rpb.shape[0]
    return t.reshape(3, h // NA_HEADS_PER_STEP, NA_HEADS_PER_STEP, NA_ROW_BLOCK * GRID_W,
                     NA_KEY_ROWS * GRID_W)


def _natten(qkv_l, qkv_c, bias):
    b, n, d3 = qkv_l.shape
    d = d3 // 3
    hw = NA_HEADS_PER_STEP * HEAD_DIM
    ng = d // hw
    ctx = qkv_c.shape[1]
    rows = n // GRID_W
    nblk = rows // NA_ROW_BLOCK
    tq = NA_ROW_BLOCK * GRID_W
    nk = NA_KEY_ROWS * GRID_W

    def bias_map(bi, h, r):
        cls = jnp.where(r == 0, 0, jnp.where(r == nblk - 1, 2, 1))
        return (cls, h, 0, 0, 0)

    return pl.pallas_call(
        functools.partial(_natten_kernel, rows=rows),
        out_shape=jax.ShapeDtypeStruct((b, n, d), BF16),
        grid=(b, ng, nblk),
        in_specs=[
            pl.BlockSpec((None, tq, hw), lambda bi, h, r: (bi, r, h)),
            pl.BlockSpec((None, n, hw), lambda bi, h, r: (bi, 0, ng + h)),
            pl.BlockSpec((None, n, hw), lambda bi, h, r: (bi, 0, 2 * ng + h)),
            pl.BlockSpec((None, ctx, hw), lambda bi, h, r: (bi, 0, ng + h)),
            pl.BlockSpec((None, ctx, hw), lambda bi, h, r: (bi, 0, 2 * ng + h)),
            pl.BlockSpec((None, None, NA_HEADS_PER_STEP, tq, nk), bias_map),
        ],
        out_specs=pl.BlockSpec((None, tq, hw), lambda bi, h, r: (bi, r, h)),
        compiler_params=_params("arbitrary", "arbitrary", "arbitrary"),
        name="natten",
    )(qkv_l, qkv_l, qkv_l, qkv_c, qkv_c, bias)


def _ctx_attn_kernel(q_ref, k_ref, v_ref, o_ref):
    s = lax.dot_general(q_ref[...], k_ref[...], (((1,), (1,)), ((), ())), preferred_element_type=F32)
    p = jnp.exp2(s - jnp.max(s, axis=-1, keepdims=True))
    l = jnp.sum(p, axis=-1, keepdims=True)
    o = jnp.dot(p.astype(BF16), v_ref[...], preferred_element_type=F32)
    o_ref[...] = (o / l).astype(o_ref.dtype)


def _ctx_attn(qkv_c):
    b, ctx, d3 = qkv_c.shape
    d = d3 // 3
    nh = d // HEAD_DIM
    return pl.pallas_call(
        _ctx_attn_kernel,
        out_shape=jax.ShapeDtypeStruct((b, ctx, d), BF16),
        grid=(b, nh),
        in_specs=[
            pl.BlockSpec((None, ctx, HEAD_DIM), lambda bi, h: (bi, 0, h)),
            pl.BlockSpec((None, ctx, HEAD_DIM), lambda bi, h: (bi, 0, nh + h)),
            pl.BlockSpec((None, ctx, HEAD_DIM), lambda bi, h: (bi, 0, 2 * nh + h)),
        ],
        out_specs=pl.BlockSpec((None, ctx, HEAD_DIM), lambda bi, h: (bi, 0, h)),
        compiler_params=_params("arbitrary", "arbitrary"),
        name="ctx_attn",
    )(qkv_c, qkv_c, qkv_c)


def _lane_tile(x, width):
    return jnp.concatenate([x] * (width // LANES), axis=1)


def _diff_kernel(q_ref, k_ref, v_ref, kc_ref, vc_ref, lq1_ref, lk1_ref, lq2_ref, lk2_ref, g_ref,
                 o_ref, m1, m2, l1, l2, a1, a2, *, tk, unroll, lambda_init):
    n = k_ref.shape[0]
    nt = (((1,), (1,)), ((), ()))
    ms, ls, accs = (m1, m2), (l1, l2), (a1, a2)
    for c in range(2):
        ms[c][...] = jnp.full(ms[c].shape, NEG_INF, F32)
        ls[c][...] = jnp.zeros(ls[c].shape, F32)
        accs[c][...] = jnp.zeros(accs[c].shape, F32)

    def step(k_blk, v_blk):
        for c in range(2):
            q = q_ref[:, c * HEAD_DIM:(c + 1) * HEAD_DIM]
            s = lax.dot_general(q, k_blk[:, c * HEAD_DIM:(c + 1) * HEAD_DIM], nt,
                                preferred_element_type=F32)
            m_prev = ms[c][...]
            m_new = jnp.maximum(m_prev, jnp.max(s, axis=-1, keepdims=True))
            alpha = jnp.exp2(m_prev - m_new)
            p = jnp.exp2(s - _lane_tile(m_new, s.shape[1]))
            ls[c][...] = alpha * ls[c][...] + jnp.sum(p, axis=-1, keepdims=True)
            pv = jnp.dot(p.astype(BF16), v_blk, preferred_element_type=F32)
            accs[c][...] = _lane_tile(alpha, pv.shape[1]) * accs[c][...] + pv
            ms[c][...] = m_new

    def body(j, carry):
        st = pl.multiple_of(j * tk, tk)
        step(k_ref[pl.ds(st, tk), :], v_ref[pl.ds(st, tk), :])
        return carry

    lax.fori_loop(0, n // tk, body, 0, unroll=unroll)
    step(kc_ref[...], vc_ref[...])

    lam = (jnp.exp(jnp.sum(lq1_ref[...] * lk1_ref[...], axis=-1, keepdims=True))
           - jnp.exp(jnp.sum(lq2_ref[...] * lk2_ref[...], axis=-1, keepdims=True)) + lambda_init)
    w = a1.shape[1]
    o = a1[...] / _lane_tile(l1[...], w) - lam * (a2[...] / _lane_tile(l2[...], w))
    o = o * lax.rsqrt(jnp.mean(o * o, axis=-1, keepdims=True) + SUBLN_EPS) * g_ref[...]
    o_ref[...] = (o * (1.0 - lambda_init)).astype(o_ref.dtype)


def _diff_attn(qkv_l, qkv_c, lq1, lk1, lq2, lk2, subln_g, lambda_init):
    b, n, d3 = qkv_l.shape
    d = d3 // 3
    hw = 2 * HEAD_DIM
    nh = d // hw
    ctx = qkv_c.shape[1]
    tq = min(DA_Q_TILE, n)
    tk = min(DA_K_TILE, n)
    vec = lambda a: a.reshape(1, -1).astype(F32)
    small = pl.BlockSpec((1, HEAD_DIM), lambda bi, h, i: (0, 0))
    return pl.pallas_call(
        functools.partial(_diff_kernel, tk=tk, unroll=min(DA_UNROLL, n // tk), lambda_init=lambda_init),
        out_shape=jax.ShapeDtypeStruct((b, n, d), BF16),
        grid=(b, nh, n // tq),
        in_specs=[
            pl.BlockSpec((None, tq, hw), lambda bi, h, i: (bi, i, h)),
            pl.BlockSpec((None, n, hw), lambda bi, h, i: (bi, 0, nh + h)),
            pl.BlockSpec((None, n, hw), lambda bi, h, i: (bi, 0, 2 * nh + h)),
            pl.BlockSpec((None, ctx, hw), lambda bi, h, i: (bi, 0, nh + h)),
            pl.BlockSpec((None, ctx, hw), lambda bi, h, i: (bi, 0, 2 * nh + h)),
            small, small, small, small,
            pl.BlockSpec((1, hw), lambda bi, h, i: (0, 0)),
        ],
        out_specs=pl.BlockSpec((None, tq, hw), lambda bi, h, i: (bi, i, h)),
        scratch_shapes=[pltpu.VMEM((tq, LANES), F32)] * 4 + [pltpu.VMEM((tq, hw), F32)] * 2,
        compiler_params=_params("arbitrary", "arbitrary", "arbitrary"),
        name="diff_attn",
    )(qkv_l, qkv_l, qkv_l, qkv_c, qkv_c, vec(lq1), vec(lk1), vec(lq2), vec(lk2), vec(subln_g))


def _ffn_kernel(x_ref, gate_ref, mod_ref, wg_ref, wu_ref, wd_ref, o_ref, a_ref, *, nf, tf,
                tiles_per_batch, mod_row):
    s = pl.program_id(2)

    @pl.when(s < nf)
    def _():
        x = x_ref[...]
        g = jnp.dot(x, wg_ref[...].astype(BF16), preferred_element_type=F32)
        u = jnp.dot(x, wu_ref[...].astype(BF16), preferred_element_type=F32)
        a = (g * (1.0 / (1.0 + jnp.exp(-g))) * u).astype(BF16)
        for f in range(nf):
            @pl.when(s == f)
            def _():
                a_ref[:, f * tf:(f + 1) * tf] = a

    @pl.when(s >= nf)
    def _():
        y = jnp.dot(a_ref[...], wd_ref[...].astype(BF16), preferred_element_type=F32)
        row = pl.program_id(1) // tiles_per_batch if mod_row is None else mod_row
        o_ref[...] = y * gate_ref[...] * mod_ref[pl.ds(row, 1), :]


def _ffn(xg, gate, mod, w_gate, w_up, w_down, layer, *, rows_per_batch, mod_row=None):
    e, r, d = xg.shape
    ff = w_gate.shape[3]
    tc = min(1024, r if mod_row is not None else rows_per_batch)
    tf = min(256, ff)
    tn = min(1024, d)
    nf = ff // tf
    nd = d // tn
    if mod_row is None:
        assert rows_per_batch % tc == 0
    tiles_per_batch = max(rows_per_batch // tc, 1)
    mod_off = (N_MOD - 1) * nd
    return pl.pallas_call(
        functools.partial(_ffn_kernel, nf=nf, tf=tf, tiles_per_batch=tiles_per_batch, mod_row=mod_row),
        out_shape=jax.ShapeDtypeStruct((e, r, d), F32),
        grid=(e, r // tc, nf + nd),
        in_specs=[
            pl.BlockSpec((None, tc, d), lambda ei, mi, s: (ei, mi, 0), pipeline_mode=pl.Buffered(1)),
            pl.BlockSpec((None, tc, 1), lambda ei, mi, s: (ei, mi, 0)),
            pl.BlockSpec((COND_ROWS, tn), lambda ei, mi, s: (0, mod_off + jnp.maximum(s - nf, 0))),
            pl.BlockSpec((None, None, d, tf), lambda ei, mi, s: (layer, ei, 0, jnp.minimum(s, nf - 1))),
            pl.BlockSpec((None, None, d, tf), lambda ei, mi, s: (layer, ei, 0, jnp.minimum(s, nf - 1))),
            pl.BlockSpec((None, None, ff, tn), lambda ei, mi, s: (layer, ei, 0, jnp.maximum(s - nf, 0))),
        ],
        out_specs=pl.BlockSpec((None, tc, tn), lambda ei, mi, s: (ei, mi, jnp.maximum(s - nf, 0))),
        scratch_shapes=[pltpu.VMEM((tc, ff), BF16)],
        compiler_params=_params("arbitrary", "arbitrary", "arbitrary"),
        name="expert_ffn",
    )(xg, gate, mod, w_gate, w_up, w_down)


def _combine_kernel(idx_ref, y_ref, x_in_ref, x_ref, buf, gsem, ssem, *, tc, cap, nb):
    del x_in_ref
    t = pl.program_id(0)
    nt = pl.num_programs(0)
    slot = t % 2

    def row_copy(chunk, sl, i, scatter):
        base = chunk * tc
        bi = (base // cap) % nb
        row = idx_ref[base + i]
        hbm = x_ref.at[bi, pl.ds(row, 1), :]
        vm = buf.at[sl, pl.ds(i, 1), :]
        return (pltpu.make_async_copy(vm, hbm, ssem.at[sl]) if scatter
                else pltpu.make_async_copy(hbm, vm, gsem.at[sl]))

    def start_all(chunk, sl, scatter):
        def body(i, carry):
            row_copy(chunk, sl, i, scatter).start()
            return carry
        lax.fori_loop(0, tc, body, 0, unroll=8)

    def wait_all(sl, scatter):
        whole = x_ref.at[0, pl.ds(0, tc), :]
        if scatter:
            pltpu.make_async_copy(buf.at[sl], whole, ssem.at[sl]).wait()
        else:
            pltpu.make_async_copy(whole, buf.at[sl], gsem.at[sl]).wait()

    @pl.when(t == 0)
    def _():
        start_all(0, 0, False)

    @pl.when(t > 0)
    def _():
        wait_all(1 - slot, True)

    @pl.when(t + 1 < nt)
    def _():
        start_all(t + 1, 1 - slot, False)

    wait_all(slot, False)
    buf[slot] = buf[slot] + y_ref[...]
    start_all(t, slot, True)

    @pl.when(t == nt - 1)
    def _():
        wait_all(slot, True)


def _combine(x, y, idx_flat, cap):
    nb, n, d = x.shape
    e, r, _ = y.shape
    tc = min(512, cap)
    assert cap % tc == 0 and (nb >= 2 or cap // tc >= 2)
    nchunk = r // tc
    return pl.pallas_call(
        functools.partial(_combine_kernel, tc=tc, cap=cap, nb=nb),
        out_shape=jax.ShapeDtypeStruct(x.shape, F32),
        grid_spec=pltpu.PrefetchScalarGridSpec(
            num_scalar_prefetch=1,
            grid=(e * nchunk,),
            in_specs=[
                pl.BlockSpec((None, tc, d), lambda t, idx: (t // nchunk, t % nchunk, 0)),
                pl.BlockSpec(memory_space=pl.ANY),
            ],
            out_specs=pl.BlockSpec(memory_space=pl.ANY),
            scratch_shapes=[pltpu.VMEM((2, tc, d), F32), pltpu.SemaphoreType.DMA((2,)),
                            pltpu.SemaphoreType.DMA((2,))],
        ),
        input_output_aliases={2: 0},
        compiler_params=pltpu.CompilerParams(dimension_semantics=("arbitrary",),
                                             vmem_limit_bytes=V7X_VMEM_LIMIT_BYTES,
                                             disable_bounds_checks=True),
        name="moe_combine",
    )(idx_flat, y, x)


def _ec_moe(x, h, aff, mod, w_gate, w_up, w_down, layer, mod_row=None):
    b, n, d = h.shape
    e = aff.shape[1]
    cap = EC_CAPACITY_FACTOR * n // e
    gate, idx = lax.top_k(aff, cap)
    idx_t = idx.transpose(1, 0, 2)
    bidx = jnp.arange(b, dtype=jnp.int32)[None, :, None]
    xg = h[bidx, idx_t].reshape(e, b * cap, d)
    gate_t = gate.transpose(1, 0, 2).reshape(e, b * cap, 1)
    y = _ffn(xg, gate_t, mod, w_gate, w_up, w_down, layer, rows_per_batch=cap, mod_row=mod_row)
    return _combine(x, y, idx_t.reshape(-1), cap)


def _rope_tables(n):
    t = np.arange(n, dtype=np.int32)
    row = (t // GRID_W).astype(np.float32)
    col = (t % GRID_W).astype(np.float32)
    pairs = HEAD_DIM // 4
    freq = jnp.asarray(ROPE_BASE, F32) ** (-jnp.arange(pairs, dtype=F32) / pairs)
    ang = jnp.concatenate([jnp.asarray(row)[:, None] * freq, jnp.asarray(col)[:, None] * freq], axis=-1)
    cos, sin = jnp.cos(ang), jnp.sin(ang)
    return (jnp.stack([cos, cos], axis=-1).reshape(n, HEAD_DIM),
            jnp.stack([-sin, sin], axis=-1).reshape(n, HEAD_DIM))


def kernel(x, c, ctx, c_ctx, ada_w, ada_b, norm1_g, norm2_g, na_w_qkv, na_w_o, na_rpb, da_w_qkv, da_w_o, da_lambda_q1, da_lambda_k1, da_lambda_q2, da_lambda_k2, da_subln_g, moe_w_router, moe_w_gate, moe_w_up, moe_w_down, final_g):
    b, n, d = x.shape
    n_ctx = ctx.shape[1]
    depth = ada_w.shape[0]
    rows = n // GRID_W
    assert n % GRID_W == 0 and rows % NA_ROW_BLOCK == 0 and rows // NA_ROW_BLOCK >= 3
    assert b + 1 <= COND_ROWS

    cond = jnp.zeros((COND_ROWS, d), F32).at[:b].set(c).at[b].set(c_ctx)
    ada_b3 = ada_b.reshape(depth, 1, N_MOD * d)
    rope = _rope_tables(n)

    for i in range(depth):
        last = i == depth - 1
        j = i // N_MIXERS
        mod = _adaln(cond, ada_w, ada_b3, i)
        h_l = _prep(x, norm1_g[i], mod, shift_chunk=0, scale_chunk=1).reshape(b * n, d)
        h_c = _prep(ctx, norm1_g[i], mod, shift_chunk=0, scale_chunk=1, mod_row=b).reshape(b * n_ctx, d)
        if i % N_MIXERS == 0:
            w_o = na_w_o
            qkv_l = _matmul(h_l, na_w_qkv, j, mode="qkv").reshape(b, n, 3 * d)
            qkv_c = _matmul(h_c, na_w_qkv, j, mode="qkv").reshape(b, n_ctx, 3 * d)
            o_l = _natten(qkv_l, qkv_c, _natten_bias(na_rpb[j], rows))
            o_c = None if last else _ctx_attn(qkv_c)
        else:
            w_o = da_w_o
            qkv_l = _matmul(h_l, da_w_qkv, j, mode="qkv", rope=rope, rows_per_batch=n).reshape(b, n, 3 * d)
            qkv_c = _matmul(h_c, da_w_qkv, j, mode="qkv").reshape(b, n_ctx, 3 * d)
            o_l = _diff_attn(qkv_l, qkv_c, da_lambda_q1[j], da_lambda_k1[j], da_lambda_q2[j],
                             da_lambda_k2[j], da_subln_g[j], _lambda_init(i))
            assert last, "context update after a differential layer is not needed at this depth"
            o_c = None
        x = _matmul(o_l.reshape(b * n, d), w_o, j, mode="resid", out_dtype=F32, resid=x.reshape(b * n, d),
                    mod=mod, gate_chunk=2, rows_per_batch=n).reshape(b, n, d)
        router_t = moe_w_router[i].T
        h2, aff = _prep(x, norm2_g[i], mod, shift_chunk=3, scale_chunk=4, router_t=router_t)
        x = _ec_moe(x, h2, aff, mod, moe_w_gate, moe_w_up, moe_w_down, i)
        if not last:
            ctx = _matmul(o_c.reshape(b * n_ctx, d), w_o, j, mode="resid", out_dtype=F32,
                          resid=ctx.reshape(b * n_ctx, d), mod=mod, gate_chunk=2,
                          rows_per_batch=n_ctx, mod_row=b).reshape(b, n_ctx, d)
            h2c, affc = _prep(ctx, norm2_g[i], mod, shift_chunk=3, scale_chunk=4, mod_row=b,
                              router_t=router_t)
            ctx = _ec_moe(ctx, h2c, affc, mod, moe_w_gate, moe_w_up, moe_w_down, i, mod_row=b)
    return _prep(x, final_g, None, out_dtype=F32)
```

```python
import functools
import math

import numpy as np
import jax
import jax.numpy as jnp
from jax import lax
from jax.experimental import pallas as pl
from jax.experimental.pallas import tpu as pltpu

F32 = jnp.float32
BF16 = jnp.bfloat16

GRID_W = 64
WIN_ROWS = 8
WIN_COLS = 16
HEAD_DIM = 128
N_MIXERS = 2
N_MOD = 6
ROPE_BASE = 10000.0
EC_CAPACITY_FACTOR = 2
NORM_EPS = 1e-6
SUBLN_EPS = 1e-5
NEG_INF = -1e30
LOG2E = math.log2(math.e)
Q_SCALE = HEAD_DIM ** -0.5 * LOG2E

V7X_VMEM_LIMIT_BYTES = 56 * 1024 * 1024
LANES = 128
COND_ROWS = 8
NA_ROW_BLOCK = 4
NA_KEY_ROWS = NA_ROW_BLOCK + WIN_ROWS - 1
NA_HEADS_PER_STEP = 2
DA_Q_TILE = 1024
DA_K_TILE = 512
DA_UNROLL = 4
MM_ROW_CHUNK = 256


def _params(*sem):
    return pltpu.CompilerParams(dimension_semantics=sem, vmem_limit_bytes=V7X_VMEM_LIMIT_BYTES)


def _lambda_init(layer_idx):
    return 0.8 - 0.6 * math.exp(-0.3 * layer_idx)


def _adaln_kernel(s_ref, w_ref, b_ref, o_ref):
    s = s_ref[...]
    s = s * (1.0 / (1.0 + jnp.exp(-s)))
    acc = jnp.dot(s.astype(BF16), w_ref[...].astype(BF16), preferred_element_type=F32)
    o_ref[...] = acc + b_ref[...]


def _adaln(cond, ada_w, ada_b3, layer):
    d = cond.shape[1]
    n6 = ada_w.shape[2]
    tn = min(512, n6)
    return pl.pallas_call(
        _adaln_kernel,
        out_shape=jax.ShapeDtypeStruct((COND_ROWS, n6), F32),
        grid=(n6 // tn,),
        in_specs=[
            pl.BlockSpec((COND_ROWS, d), lambda j: (0, 0)),
            pl.BlockSpec((None, d, tn), lambda j: (layer, 0, j)),
            pl.BlockSpec((None, 1, tn), lambda j: (layer, 0, j)),
        ],
        out_specs=pl.BlockSpec((COND_ROWS, tn), lambda j: (0, j)),
        compiler_params=_params("arbitrary"),
        name="adaln",
    )(cond, ada_w, ada_b3)


def _prep_kernel(x_ref, g_ref, *rest, eps, mod_row, modulate, with_router):
    rest = list(rest)
    if modulate:
        sh_ref, sc_ref = rest[:2]
        rest = rest[2:]
    if with_router:
        wr_ref, o_ref, aff_ref = rest
    else:
        (o_ref,) = rest
    x = x_ref[...]
    h = x * lax.rsqrt(jnp.mean(x * x, axis=-1, keepdims=True) + eps) * g_ref[...]
    if modulate:
        row = pl.program_id(0) if mod_row is None else mod_row
        h = h * (1.0 + sc_ref[pl.ds(row, 1), :]) + sh_ref[pl.ds(row, 1), :]
    if o_ref.dtype == jnp.uint32:
        bits = pltpu.bitcast(h.astype(BF16).astype(F32), jnp.uint32)
        half = o_ref.shape[1]
        o_ref[...] = (bits[:, :half] >> 16) | bits[:, half:]
    else:
        o_ref[...] = h.astype(o_ref.dtype)
    if with_router:
        logits = lax.dot_general(wr_ref[...], h, (((1,), (1,)), ((), ())),
                                 precision=lax.Precision.HIGHEST, preferred_element_type=F32)
        e = jnp.exp(logits - jnp.max(logits, axis=0, keepdims=True))
        aff_ref[...] = e / jnp.sum(e, axis=0, keepdims=True)


def _prep(x, g, mod, *, shift_chunk=None, scale_chunk=None, mod_row=None, out_dtype=BF16,
          router_t=None, eps=NORM_EPS):
    b, n, d = x.shape
    tm = min(256, n)
    modulate = shift_chunk is not None
    with_router = router_t is not None
    in_specs = [pl.BlockSpec((None, tm, d), lambda bi, i: (bi, i, 0)),
                pl.BlockSpec((1, d), lambda bi, i: (0, 0))]
    args = [x, g.reshape(1, d)]
    if modulate:
        in_specs += [pl.BlockSpec((COND_ROWS, d), lambda bi, i: (0, shift_chunk)),
                     pl.BlockSpec((COND_ROWS, d), lambda bi, i: (0, scale_chunk))]
        args += [mod, mod]
    d_out = d // 2 if out_dtype == jnp.uint32 else d
    out_shape = [jax.ShapeDtypeStruct((b, n, d_out), out_dtype)]
    out_specs = [pl.BlockSpec((None, tm, d_out), lambda bi, i: (bi, i, 0))]
    if with_router:
        e = router_t.shape[0]
        in_specs.append(pl.BlockSpec((e, d), lambda bi, i: (0, 0)))
        args.append(router_t)
        out_shape.append(jax.ShapeDtypeStruct((b, e, n), F32))
        out_specs.append(pl.BlockSpec((None, e, tm), lambda bi, i: (bi, 0, i)))
    out = pl.pallas_call(
        functools.partial(_prep_kernel, eps=eps, mod_row=mod_row, modulate=modulate,
                          with_router=with_router),
        out_shape=out_shape,
        grid=(b, n // tm),
        in_specs=in_specs,
        out_specs=out_specs,
        compiler_params=_params("arbitrary", "arbitrary"),
        name="prep",
    )(*args)
    return out if with_router else out[0]


def _mm_kernel(a_ref, w_ref, *rest, mode, rope, tiles_per_batch, mod_row):
    o_ref, wb_ref = rest[-2:]
    i = pl.program_id(1)

    @pl.when(i == 0)
    def _():
        wb_ref[...] = w_ref[...].astype(BF16)

    if mode == "resid":
        acc = jnp.dot(a_ref[...], wb_ref[...], preferred_element_type=F32)
        x_ref, gate_ref = rest[:2]
        row = i // tiles_per_batch if mod_row is None else mod_row
        o_ref[...] = x_ref[...] + gate_ref[pl.ds(row, 1), :] * acc
        return

    chunk = min(MM_ROW_CHUNK, o_ref.shape[0])
    if rope:
        even = lax.broadcasted_iota(jnp.int32, (chunk, LANES), 1) % 2 == 0
    else:
        scale = rest[0][...]
    for r in range(o_ref.shape[0] // chunk):
        rows = slice(r * chunk, (r + 1) * chunk)
        acc = jnp.dot(a_ref[rows, :], wb_ref[...], preferred_element_type=F32)
        if rope:
            cos = rest[0][rows, :]
            sin = rest[1][rows, :]
        for c in range(o_ref.shape[1] // LANES):
            xc = acc[:, c * LANES:(c + 1) * LANES]
            if rope:
                partner = jnp.where(even, pltpu.roll(xc, LANES - 1, 1), pltpu.roll(xc, 1, 1))
                xc = xc * cos + partner * sin
            else:
                xc = xc * scale
            o_ref[rows, c * LANES:(c + 1) * LANES] = xc.astype(o_ref.dtype)


def _matmul(a, w, layer, *, mode, out_dtype=BF16, tables=None, resid=None, mod=None, gate_chunk=None,
            rows_per_batch=None, mod_row=None):
    m, k = a.shape
    n = w.shape[2]
    tm = min(1024, m)
    tn = min(512, n)
    if rows_per_batch is not None:
        tm = min(tm, rows_per_batch)
    in_specs = [pl.BlockSpec((tm, k), lambda j, i: (i, 0)),
                pl.BlockSpec((None, k, tn), lambda j, i: (layer, 0, j))]
    args = [a, w]
    tiles_per_batch = 1
    rope = isinstance(tables, tuple)
    if mode == "qkv":
        nq_tiles = n // 3 // tn
        if rope:
            nblk = tables[0].shape[1] // tm
            spec = pl.BlockSpec((None, tm, LANES), lambda j, i: (j // nq_tiles, i % nblk, 0))
            in_specs += [spec, spec]
            args += list(tables)
        else:
            in_specs.append(pl.BlockSpec((None, 1, LANES), lambda j, i: (j // nq_tiles, 0, 0)))
            args.append(tables)
    else:
        tiles_per_batch = rows_per_batch // tm
        gate_off = gate_chunk * (n // tn)
        in_specs += [pl.BlockSpec((tm, tn), lambda j, i: (i, j)),
                     pl.BlockSpec((COND_ROWS, tn), lambda j, i: (0, gate_off + j))]
        args += [resid, mod]
    return pl.pallas_call(
        functools.partial(_mm_kernel, mode=mode, rope=rope, tiles_per_batch=tiles_per_batch,
                          mod_row=mod_row),
        out_shape=jax.ShapeDtypeStruct((m, n), out_dtype),
        grid=(n // tn, m // tm),
        in_specs=in_specs,
        out_specs=pl.BlockSpec((tm, tn), lambda j, i: (i, j)),
        scratch_shapes=[pltpu.VMEM((k, tn), BF16)],
        compiler_params=_params("arbitrary", "arbitrary"),
        name="matmul_" + mode,
    )(*args)


def _natten_kernel(q_ref, k_ref, v_ref, kc_ref, vc_ref, bias_ref, o_ref, *, rows):
    r = pl.program_id(2)
    ws = jnp.clip(r * NA_ROW_BLOCK - WIN_ROWS // 2, 0, rows - NA_KEY_ROWS)
    start = pl.multiple_of(ws * GRID_W, GRID_W)
    nk = NA_KEY_ROWS * GRID_W
    nt = (((1,), (1,)), ((), ()))
    for h in range(NA_HEADS_PER_STEP):
        sl = slice(h * HEAD_DIM, (h + 1) * HEAD_DIM)
        q = q_ref[:, sl]
        k = k_ref[pl.ds(start, nk), sl]
        v = v_ref[pl.ds(start, nk), sl]
        s = lax.dot_general(q, k, nt, preferred_element_type=F32) + bias_ref[h]
        sc = lax.dot_general(q, kc_ref[:, sl], nt, preferred_element_type=F32)
        m = jnp.maximum(jnp.max(s, axis=-1, keepdims=True), jnp.max(sc, axis=-1, keepdims=True))
        p = jnp.exp2(s - m)
        pc = jnp.exp2(sc - m)
        l = jnp.sum(p, axis=-1, keepdims=True) + jnp.sum(pc, axis=-1, keepdims=True)
        o = (jnp.dot(p.astype(BF16), v, preferred_element_type=F32)
             + jnp.dot(pc.astype(BF16), vc_ref[:, sl], preferred_element_type=F32))
        o_ref[:, sl] = (o / l).astype(o_ref.dtype)


def _natten_bias(rpb, rows):
    nblk = rows // NA_ROW_BLOCK
    kh = WIN_ROWS
    qc = np.arange(GRID_W, dtype=np.int32)
    col_start = np.clip(qc - WIN_COLS // 2, 0, GRID_W - WIN_COLS)
    col_mask = (qc[None, :] >= col_start[:, None]) & (qc[None, :] < col_start[:, None] + WIN_COLS)
    col_idx = np.clip(qc[None, :] - qc[:, None] + WIN_COLS - 1, 0, 2 * WIN_COLS - 2).astype(np.int32)
    row_idx = np.zeros((3, NA_ROW_BLOCK, NA_KEY_ROWS), np.int32)
    valid = np.zeros((3, NA_ROW_BLOCK, NA_KEY_ROWS), bool)
    for cls, blk in enumerate((0, 1, nblk - 1)):
        r0 = blk * NA_ROW_BLOCK
        ws = int(np.clip(r0 - WIN_ROWS // 2, 0, rows - NA_KEY_ROWS))
        for a in range(NA_ROW_BLOCK):
            r = r0 + a
            rs = int(np.clip(r - kh // 2, 0, rows - kh))
            for kr in range(NA_KEY_ROWS):
                key_row = ws + kr
                if rs <= key_row < rs + kh:
                    valid[cls, a, kr] = True
                    row_idx[cls, a, kr] = key_row - r + (WIN_ROWS - 1)
    for blk in range(1, nblk - 1):
        r0 = blk * NA_ROW_BLOCK
        ws = int(np.clip(r0 - WIN_ROWS // 2, 0, rows - NA_KEY_ROWS))
        for a in range(NA_ROW_BLOCK):
            rs = int(np.clip(r0 + a - kh // 2, 0, rows - kh))
            assert rs - ws == a and r0 - ws == WIN_ROWS // 2
    t = rpb[:, row_idx]
    t = t[..., col_idx]
    mask = valid[None, :, :, :, None, None] & col_mask[None, None, None, None]
    t = jnp.where(mask, t.astype(F32) * LOG2E, NEG_INF)
    t = t.transpose(1, 0, 2, 4, 3, 5)
    h = rpb.shape[0]
    return t.reshape(3, h // NA_HEADS_PER_STEP, NA_HEADS_PER_STEP, NA_ROW_BLOCK * GRID_W,
                     NA_KEY_ROWS * GRID_W)


def _natten(qkv_l, qkv_c, bias):
    b, n, d3 = qkv_l.shape
    d = d3 // 3
    hw = NA_HEADS_PER_STEP * HEAD_DIM
    ng = d // hw
    ctx = qkv_c.shape[1]
    rows = n // GRID_W
    nblk = rows // NA_ROW_BLOCK
    tq = NA_ROW_BLOCK * GRID_W
    nk = NA_KEY_ROWS * GRID_W

    def bias_map(bi, h, r):
        cls = jnp.where(r == 0, 0, jnp.where(r == nblk - 1, 2, 1))
        return (cls, h, 0, 0, 0)

    return pl.pallas_call(
        functools.partial(_natten_kernel, rows=rows),
        out_shape=jax.ShapeDtypeStruct((b, n, d), BF16),
        grid=(b, ng, nblk),
        in_specs=[
            pl.BlockSpec((None, tq, hw), lambda bi, h, r: (bi, r, h)),
            pl.BlockSpec((None, n, hw), lambda bi, h, r: (bi, 0, ng + h)),
            pl.BlockSpec((None, n, hw), lambda bi, h, r: (bi, 0, 2 * ng + h)),
            pl.BlockSpec((None, ctx, hw), lambda bi, h, r: (bi, 0, ng + h)),
            pl.BlockSpec((None, ctx, hw), lambda bi, h, r: (bi, 0, 2 * ng + h)),
            pl.BlockSpec((None, None, NA_HEADS_PER_STEP, tq, nk), bias_map),
        ],
        out_specs=pl.BlockSpec((None, tq, hw), lambda bi, h, r: (bi, r, h)),
        compiler_params=_params("arbitrary", "arbitrary", "arbitrary"),
        name="natten",
    )(qkv_l, qkv_l, qkv_l, qkv_c, qkv_c, bias)


def _ctx_attn_kernel(q_ref, k_ref, v_ref, o_ref):
    s = lax.dot_general(q_ref[...], k_ref[...], (((1,), (1,)), ((), ())), preferred_element_type=F32)
    p = jnp.exp2(s - jnp.max(s, axis=-1, keepdims=True))
    l = jnp.sum(p, axis=-1, keepdims=True)
    o = jnp.dot(p.astype(BF16), v_ref[...], preferred_element_type=F32)
    o_ref[...] = (o / l).astype(o_ref.dtype)


def _ctx_attn(qkv_c):
    b, ctx, d3 = qkv_c.shape
    d = d3 // 3
    nh = d // HEAD_DIM
    return pl.pallas_call(
        _ctx_attn_kernel,
        out_shape=jax.ShapeDtypeStruct((b, ctx, d), BF16),
        grid=(b, nh),
        in_specs=[
            pl.BlockSpec((None, ctx, HEAD_DIM), lambda bi, h: (bi, 0, h)),
            pl.BlockSpec((None, ctx, HEAD_DIM), lambda bi, h: (bi, 0, nh + h)),
            pl.BlockSpec((None, ctx, HEAD_DIM), lambda bi, h: (bi, 0, 2 * nh + h)),
        ],
        out_specs=pl.BlockSpec((None, ctx, HEAD_DIM), lambda bi, h: (bi, 0, h)),
        compiler_params=_params("arbitrary", "arbitrary"),
        name="ctx_attn",
    )(qkv_c, qkv_c, qkv_c)


def _lane_tile(x, width):
    return jnp.concatenate([x] * (width // LANES), axis=1)


def _diff_kernel(q_ref, k_ref, v_ref, kc_ref, vc_ref, lq1_ref, lk1_ref, lq2_ref, lk2_ref, g_ref,
                 o_ref, m1, m2, l1, l2, a1, a2, *, tk, unroll, lambda_init):
    n = k_ref.shape[0]
    nt = (((1,), (1,)), ((), ()))
    ms, ls, accs = (m1, m2), (l1, l2), (a1, a2)
    for c in range(2):
        ms[c][...] = jnp.full(ms[c].shape, NEG_INF, F32)
        ls[c][...] = jnp.zeros(ls[c].shape, F32)
        accs[c][...] = jnp.zeros(accs[c].shape, F32)

    def step(k_blk, v_blk):
        for c in range(2):
            q = q_ref[:, c * HEAD_DIM:(c + 1) * HEAD_DIM]
            s = lax.dot_general(q, k_blk[:, c * HEAD_DIM:(c + 1) * HEAD_DIM], nt,
                                preferred_element_type=F32)
            m_prev = ms[c][...]
            m_new = jnp.maximum(m_prev, jnp.max(s, axis=-1, keepdims=True))
            alpha = jnp.exp2(m_prev - m_new)
            p = jnp.exp2(s - _lane_tile(m_new, s.shape[1]))
            ls[c][...] = alpha * ls[c][...] + jnp.sum(p, axis=-1, keepdims=True)
            pv = jnp.dot(p.astype(BF16), v_blk, preferred_element_type=F32)
            accs[c][...] = _lane_tile(alpha, pv.shape[1]) * accs[c][...] + pv
            ms[c][...] = m_new

    def body(j, carry):
        st = pl.multiple_of(j * tk, tk)
        step(k_ref[pl.ds(st, tk), :], v_ref[pl.ds(st, tk), :])
        return carry

    lax.fori_loop(0, n // tk, body, 0, unroll=unroll)
    step(kc_ref[...], vc_ref[...])

    lam = (jnp.exp(jnp.sum(lq1_ref[...] * lk1_ref[...], axis=-1, keepdims=True))
           - jnp.exp(jnp.sum(lq2_ref[...] * lk2_ref[...], axis=-1, keepdims=True)) + lambda_init)
    w = a1.shape[1]
    o = a1[...] / _lane_tile(l1[...], w) - lam * (a2[...] / _lane_tile(l2[...], w))
    o = o * lax.rsqrt(jnp.mean(o * o, axis=-1, keepdims=True) + SUBLN_EPS) * g_ref[...]
    o_ref[...] = (o * (1.0 - lambda_init)).astype(o_ref.dtype)


def _diff_attn(qkv_l, qkv_c, lq1, lk1, lq2, lk2, subln_g, lambda_init):
    b, n, d3 = qkv_l.shape
    d = d3 // 3
    hw = 2 * HEAD_DIM
    nh = d // hw
    ctx = qkv_c.shape[1]
    tq = min(DA_Q_TILE, n)
    tk = min(DA_K_TILE, n)
    vec = lambda a: a.reshape(1, -1).astype(F32)
    small = pl.BlockSpec((1, HEAD_DIM), lambda bi, h, i: (0, 0))
    return pl.pallas_call(
        functools.partial(_diff_kernel, tk=tk, unroll=min(DA_UNROLL, n // tk), lambda_init=lambda_init),
        out_shape=jax.ShapeDtypeStruct((b, n, d), BF16),
        grid=(b, nh, n // tq),
        in_specs=[
            pl.BlockSpec((None, tq, hw), lambda bi, h, i: (bi, i, h)),
            pl.BlockSpec((None, n, hw), lambda bi, h, i: (bi, 0, nh + h)),
            pl.BlockSpec((None, n, hw), lambda bi, h, i: (bi, 0, 2 * nh + h)),
            pl.BlockSpec((None, ctx, hw), lambda bi, h, i: (bi, 0, nh + h)),
            pl.BlockSpec((None, ctx, hw), lambda bi, h, i: (bi, 0, 2 * nh + h)),
            small, small, small, small,
            pl.BlockSpec((1, hw), lambda bi, h, i: (0, 0)),
        ],
        out_specs=pl.BlockSpec((None, tq, hw), lambda bi, h, i: (bi, i, h)),
        scratch_shapes=[pltpu.VMEM((tq, LANES), F32)] * 4 + [pltpu.VMEM((tq, hw), F32)] * 2,
        compiler_params=_params("arbitrary", "arbitrary", "arbitrary"),
        name="diff_attn",
    )(qkv_l, qkv_l, qkv_l, qkv_c, qkv_c, vec(lq1), vec(lk1), vec(lq2), vec(lk2), vec(subln_g))


def _ffn_kernel(idx_ref, h_ref, gate_ref, mod_ref, wg_ref, wu_ref, wd_ref, o_ref,
                gbuf, xlo, xhi, a_ref, gsem, *, nf, tf, tc, tiles_per_batch, mod_row):
    m = pl.program_id(1)
    s = pl.program_id(2)
    nm = pl.num_programs(1)
    tile = pl.program_id(0) * nm + m
    ntiles = pl.num_programs(0) * nm

    def start_gather(tl):
        base = tl * tc

        def body(i, carry):
            src = h_ref.at[pl.ds(idx_ref[base + i], 1), :]
            pltpu.make_async_copy(src, gbuf.at[pl.ds(i, 1), :], gsem).start()
            return carry

        lax.fori_loop(0, tc, body, 0, unroll=8)

    @pl.when((tile == 0) & (s == 0))
    def _():
        start_gather(0)

    @pl.when(s == 0)
    def _():
        pltpu.make_async_copy(h_ref.at[pl.ds(0, tc), :], gbuf, gsem).wait()
        w = gbuf[...]
        xlo[...] = pltpu.bitcast(w << 16, F32).astype(BF16)
        xhi[...] = pltpu.bitcast(w & jnp.uint32(0xFFFF0000), F32).astype(BF16)

    @pl.when((s == nf) & (tile + 1 < ntiles))
    def _():
        start_gather(tile + 1)

    @pl.when(s < nf)
    def _():
        half = xlo.shape[1]
        wg = wg_ref[...].astype(BF16)
        wu = wu_ref[...].astype(BF16)
        g = (jnp.dot(xlo[...], wg[:half], preferred_element_type=F32)
             + jnp.dot(xhi[...], wg[half:], preferred_element_type=F32))
        u = (jnp.dot(xlo[...], wu[:half], preferred_element_type=F32)
             + jnp.dot(xhi[...], wu[half:], preferred_element_type=F32))
        a = (g * (1.0 / (1.0 + jnp.exp(-g))) * u).astype(BF16)
        for f in range(nf):
            @pl.when(s == f)
            def _():
                a_ref[:, f * tf:(f + 1) * tf] = a

    @pl.when(s >= nf)
    def _():
        y = jnp.dot(a_ref[...], wd_ref[...].astype(BF16), preferred_element_type=F32)
        row = m // tiles_per_batch if mod_row is None else mod_row
        o_ref[...] = y * gate_ref[...] * mod_ref[pl.ds(row, 1), :]


def _ffn(h_packed, idx_flat, gate, mod, w_gate, w_up, w_down, layer, *, cap, mod_row=None):
    n, half = h_packed.shape
    d = 2 * half
    e, r, _ = gate.shape
    ff = w_gate.shape[3]
    tc = min(1024, r if mod_row is not None else cap)
    tf = min(256, ff)
    tn = min(1024, d)
    nf = ff // tf
    nd = d // tn
    assert r % tc == 0 and tc <= n and (mod_row is not None or cap % tc == 0)
    tiles_per_batch = max(cap // tc, 1)
    mod_off = (N_MOD - 1) * nd
    return pl.pallas_call(
        functools.partial(_ffn_kernel, nf=nf, tf=tf, tc=tc, tiles_per_batch=tiles_per_batch,
                          mod_row=mod_row),
        out_shape=jax.ShapeDtypeStruct((e, r, d), F32),
        grid_spec=pltpu.PrefetchScalarGridSpec(
            num_scalar_prefetch=1,
            grid=(e, r // tc, nf + nd),
            in_specs=[
                pl.BlockSpec(memory_space=pl.ANY),
                pl.BlockSpec((None, tc, 1), lambda ei, mi, s, idx: (ei, mi, 0)),
                pl.BlockSpec((COND_ROWS, tn), lambda ei, mi, s, idx: (0, mod_off + jnp.maximum(s - nf, 0))),
                pl.BlockSpec((None, None, d, tf),
                             lambda ei, mi, s, idx: (layer, ei, 0, jnp.minimum(s, nf - 1))),
                pl.BlockSpec((None, None, d, tf),
                             lambda ei, mi, s, idx: (layer, ei, 0, jnp.minimum(s, nf - 1))),
                pl.BlockSpec((None, None, ff, tn),
                             lambda ei, mi, s, idx: (layer, ei, 0, jnp.maximum(s - nf, 0))),
            ],
            out_specs=pl.BlockSpec((None, tc, tn), lambda ei, mi, s, idx: (ei, mi, jnp.maximum(s - nf, 0))),
            scratch_shapes=[pltpu.VMEM((tc, half), jnp.uint32), pltpu.VMEM((tc, half), BF16),
                            pltpu.VMEM((tc, half), BF16), pltpu.VMEM((tc, ff), BF16),
                            pltpu.SemaphoreType.DMA(())],
        ),
        compiler_params=pltpu.CompilerParams(dimension_semantics=("arbitrary",) * 3,
                                             vmem_limit_bytes=V7X_VMEM_LIMIT_BYTES,
                                             disable_bounds_checks=True),
        name="expert_ffn",
    )(idx_flat, h_packed, gate, mod, w_gate, w_up, w_down)


def _combine_kernel(idx_ref, y_ref, x_in_ref, x_ref, buf, gsem, ssem, *, tc):
    del x_in_ref
    t = pl.program_id(0)
    nt = pl.num_programs(0)
    slot = t % 2

    def start_all(chunk, sl, scatter):
        base = chunk * tc

        def body(i, carry):
            hbm = x_ref.at[pl.ds(idx_ref[base + i], 1), :]
            vm = buf.at[sl, pl.ds(i, 1), :]
            if scatter:
                pltpu.make_async_copy(vm, hbm, ssem.at[sl]).start()
            else:
                pltpu.make_async_copy(hbm, vm, gsem.at[sl]).start()
            return carry

        lax.fori_loop(0, tc, body, 0, unroll=8)

    def wait_all(sl, scatter):
        whole = x_ref.at[pl.ds(0, tc), :]
        if scatter:
            pltpu.make_async_copy(buf.at[sl], whole, ssem.at[sl]).wait()
        else:
            pltpu.make_async_copy(whole, buf.at[sl], gsem.at[sl]).wait()

    @pl.when(t == 0)
    def _():
        start_all(0, 0, False)

    @pl.when(t > 0)
    def _():
        wait_all(1 - slot, True)

    @pl.when(t + 1 < nt)
    def _():
        start_all(t + 1, 1 - slot, False)

    wait_all(slot, False)
    buf[slot] = buf[slot] + y_ref[...]
    start_all(t, slot, True)

    @pl.when(t == nt - 1)
    def _():
        wait_all(slot, True)


def _combine(x, y, idx_flat, cap):
    d = x.shape[1]
    e, r, _ = y.shape
    tc = min(512, cap)
    assert cap % tc == 0 and (r // cap >= 2 or cap // tc >= 2)
    nchunk = r // tc
    return pl.pallas_call(
        functools.partial(_combine_kernel, tc=tc),
        out_shape=jax.ShapeDtypeStruct(x.shape, F32),
        grid_spec=pltpu.PrefetchScalarGridSpec(
            num_scalar_prefetch=1,
            grid=(e * nchunk,),
            in_specs=[
                pl.BlockSpec((None, tc, d), lambda t, idx: (t // nchunk, t % nchunk, 0)),
                pl.BlockSpec(memory_space=pl.ANY),
            ],
            out_specs=pl.BlockSpec(memory_space=pl.ANY),
            scratch_shapes=[pltpu.VMEM((2, tc, d), F32), pltpu.SemaphoreType.DMA((2,)),
                            pltpu.SemaphoreType.DMA((2,))],
        ),
        input_output_aliases={2: 0},
        compiler_params=pltpu.CompilerParams(dimension_semantics=("arbitrary",),
                                             vmem_limit_bytes=V7X_VMEM_LIMIT_BYTES,
                                             disable_bounds_checks=True),
        name="moe_combine",
    )(idx_flat, y, x)


def _ec_moe(x, h, aff, mod, w_gate, w_up, w_down, layer, mod_row=None):
    b, n, d = x.shape
    e = aff.shape[1]
    cap = EC_CAPACITY_FACTOR * n // e
    gate, idx = lax.top_k(aff, cap)
    rows = idx + (jnp.arange(b, dtype=jnp.int32) * n)[:, None, None]
    idx_flat = rows.transpose(1, 0, 2).reshape(-1)
    gate_t = gate.transpose(1, 0, 2).reshape(e, b * cap, 1)
    y = _ffn(h.reshape(b * n, d // 2), idx_flat, gate_t, mod, w_gate, w_up, w_down, layer, cap=cap,
             mod_row=mod_row)
    return _combine(x.reshape(b * n, d), y, idx_flat, cap).reshape(b, n, d)


def _rope_tables(n):
    t = np.arange(n, dtype=np.int32)
    row = (t // GRID_W).astype(np.float32)
    col = (t % GRID_W).astype(np.float32)
    pairs = HEAD_DIM // 4
    freq = jnp.asarray(ROPE_BASE, F32) ** (-jnp.arange(pairs, dtype=F32) / pairs)
    ang = jnp.concatenate([jnp.asarray(row)[:, None] * freq, jnp.asarray(col)[:, None] * freq], axis=-1)
    cos, sin = jnp.cos(ang), jnp.sin(ang)
    cos = jnp.stack([cos, cos], axis=-1).reshape(n, HEAD_DIM)
    sin = jnp.stack([-sin, sin], axis=-1).reshape(n, HEAD_DIM)
    return (jnp.stack([cos * Q_SCALE, cos, jnp.ones_like(cos)]),
            jnp.stack([sin * Q_SCALE, sin, jnp.zeros_like(sin)]))


def _scale_table():
    return jnp.broadcast_to(jnp.asarray([Q_SCALE, 1.0, 1.0], F32)[:, None, None], (3, 1, LANES))


def kernel(x, c, ctx, c_ctx, ada_w, ada_b, norm1_g, norm2_g, na_w_qkv, na_w_o, na_rpb, da_w_qkv, da_w_o, da_lambda_q1, da_lambda_k1, da_lambda_q2, da_lambda_k2, da_subln_g, moe_w_router, moe_w_gate, moe_w_up, moe_w_down, final_g):
    b, n, d = x.shape
    n_ctx = ctx.shape[1]
    depth = ada_w.shape[0]
    rows = n // GRID_W
    assert n % GRID_W == 0 and rows % NA_ROW_BLOCK == 0 and rows // NA_ROW_BLOCK >= 3
    assert b + 1 <= COND_ROWS

    cond = jnp.zeros((COND_ROWS, d), F32).at[:b].set(c).at[b].set(c_ctx)
    ada_b3 = ada_b.reshape(depth, 1, N_MOD * d)
    rope = _rope_tables(n)
    plain = _scale_table()

    for i in range(depth):
        last = i == depth - 1
        j = i // N_MIXERS
        mod = _adaln(cond, ada_w, ada_b3, i)
        h_l = _prep(x, norm1_g[i], mod, shift_chunk=0, scale_chunk=1).reshape(b * n, d)
        h_c = _prep(ctx, norm1_g[i], mod, shift_chunk=0, scale_chunk=1, mod_row=b).reshape(b * n_ctx, d)
        if i % N_MIXERS == 0:
            w_o = na_w_o
            qkv_l = _matmul(h_l, na_w_qkv, j, mode="qkv", tables=plain).reshape(b, n, 3 * d)
            qkv_c = _matmul(h_c, na_w_qkv, j, mode="qkv", tables=plain).reshape(b, n_ctx, 3 * d)
            o_l = _natten(qkv_l, qkv_c, _natten_bias(na_rpb[j], rows))
            o_c = None if last else _ctx_attn(qkv_c)
        else:
            w_o = da_w_o
            qkv_l = _matmul(h_l, da_w_qkv, j, mode="qkv", tables=rope, rows_per_batch=n).reshape(b, n, 3 * d)
            qkv_c = _matmul(h_c, da_w_qkv, j, mode="qkv", tables=plain).reshape(b, n_ctx, 3 * d)
            o_l = _diff_attn(qkv_l, qkv_c, da_lambda_q1[j], da_lambda_k1[j], da_lambda_q2[j],
                             da_lambda_k2[j], da_subln_g[j], _lambda_init(i))
            assert last, "context update after a differential layer is not needed at this depth"
            o_c = None
        x = _matmul(o_l.reshape(b * n, d), w_o, j, mode="resid", out_dtype=F32, resid=x.reshape(b * n, d),
                    mod=mod, gate_chunk=2, rows_per_batch=n).reshape(b, n, d)
        router_t = moe_w_router[i].T
        h2, aff = _prep(x, norm2_g[i], mod, shift_chunk=3, scale_chunk=4, router_t=router_t,
                        out_dtype=jnp.uint32)
        x = _ec_moe(x, h2, aff, mod, moe_w_gate, moe_w_up, moe_w_down, i)
        if not last:
            ctx = _matmul(o_c.reshape(b * n_ctx, d), w_o, j, mode="resid", out_dtype=F32,
                          resid=ctx.reshape(b * n_ctx, d), mod=mod, gate_chunk=2,
                          rows_per_batch=n_ctx, mod_row=b).reshape(b, n_ctx, d)
            h2c, affc = _prep(ctx, norm2_g[i], mod, shift_chunk=3, scale_chunk=4, mod_row=b,
                              router_t=router_t, out_dtype=jnp.uint32)
            ctx = _ec_moe(ctx, h2c, affc, mod, moe_w_gate, moe_w_up, moe_w_down, i, mod_row=b)
    return _prep(x, final_g, None, out_dtype=F32)
```

```python
import functools
import math

import numpy as np
import jax
import jax.numpy as jnp
from jax import lax
from jax.experimental import pallas as pl
from jax.experimental.pallas import tpu as pltpu

F32 = jnp.float32
BF16 = jnp.bfloat16

GRID_W = 64
WIN_ROWS = 8
WIN_COLS = 16
HEAD_DIM = 128
N_MIXERS = 2
N_MOD = 6
ROPE_BASE = 10000.0
EC_CAPACITY_FACTOR = 2
NORM_EPS = 1e-6
SUBLN_EPS = 1e-5
NEG_INF = -1e30
LOG2E = math.log2(math.e)
Q_SCALE = HEAD_DIM ** -0.5 * LOG2E

V7X_VMEM_LIMIT_BYTES = 56 * 1024 * 1024
LANES = 128
COND_ROWS = 8
NA_ROW_BLOCK = 4
NA_KEY_ROWS = NA_ROW_BLOCK + WIN_ROWS - 1
NA_HEADS_PER_STEP = 2
DA_Q_TILE = 1024
DA_K_TILE = 512
DA_UNROLL = 4
MM_ROW_CHUNK = 256


def _params(*sem):
    return pltpu.CompilerParams(dimension_semantics=sem, vmem_limit_bytes=V7X_VMEM_LIMIT_BYTES)


def _lambda_init(layer_idx):
    return 0.8 - 0.6 * math.exp(-0.3 * layer_idx)


def _adaln_kernel(s_ref, w_ref, b_ref, o_ref):
    s = s_ref[...]
    s = s * (1.0 / (1.0 + jnp.exp(-s)))
    acc = jnp.dot(s.astype(BF16), w_ref[...].astype(BF16), preferred_element_type=F32)
    o_ref[...] = acc + b_ref[...]


def _adaln(cond, ada_w, ada_b3, layer):
    d = cond.shape[1]
    n6 = ada_w.shape[2]
    tn = min(512, n6)
    return pl.pallas_call(
        _adaln_kernel,
        out_shape=jax.ShapeDtypeStruct((COND_ROWS, n6), F32),
        grid=(n6 // tn,),
        in_specs=[
            pl.BlockSpec((COND_ROWS, d), lambda j: (0, 0)),
            pl.BlockSpec((None, d, tn), lambda j: (layer, 0, j)),
            pl.BlockSpec((None, 1, tn), lambda j: (layer, 0, j)),
        ],
        out_specs=pl.BlockSpec((COND_ROWS, tn), lambda j: (0, j)),
        compiler_params=_params("arbitrary"),
        name="adaln",
    )(cond, ada_w, ada_b3)


def _prep_kernel(x_ref, g_ref, *rest, eps, mod_row, modulate, with_router):
    rest = list(rest)
    if modulate:
        sh_ref, sc_ref = rest[:2]
        rest = rest[2:]
    if with_router:
        wr_ref, o_ref, aff_ref = rest
    else:
        (o_ref,) = rest
    x = x_ref[...]
    h = x * lax.rsqrt(jnp.mean(x * x, axis=-1, keepdims=True) + eps) * g_ref[...]
    if modulate:
        row = pl.program_id(0) if mod_row is None else mod_row
        h = h * (1.0 + sc_ref[pl.ds(row, 1), :]) + sh_ref[pl.ds(row, 1), :]
    if o_ref.dtype == jnp.uint32:
        bits = pltpu.bitcast(h.astype(BF16).astype(F32), jnp.uint32)
        half = o_ref.shape[1]
        o_ref[...] = (bits[:, :half] >> 16) | bits[:, half:]
    else:
        o_ref[...] = h.astype(o_ref.dtype)
    if with_router:
        logits = lax.dot_general(wr_ref[...], h, (((1,), (1,)), ((), ())),
                                 precision=lax.Precision.HIGHEST, preferred_element_type=F32)
        e = jnp.exp(logits - jnp.max(logits, axis=0, keepdims=True))
        aff_ref[...] = e / jnp.sum(e, axis=0, keepdims=True)


def _prep(x, g, mod, *, shift_chunk=None, scale_chunk=None, mod_row=None, out_dtype=BF16,
          router_t=None, eps=NORM_EPS):
    b, n, d = x.shape
    tm = min(256, n)
    modulate = shift_chunk is not None
    with_router = router_t is not None
    in_specs = [pl.BlockSpec((None, tm, d), lambda bi, i: (bi, i, 0)),
                pl.BlockSpec((1, d), lambda bi, i: (0, 0))]
    args = [x, g.reshape(1, d)]
    if modulate:
        in_specs += [pl.BlockSpec((COND_ROWS, d), lambda bi, i: (0, shift_chunk)),
                     pl.BlockSpec((COND_ROWS, d), lambda bi, i: (0, scale_chunk))]
        args += [mod, mod]
    d_out = d // 2 if out_dtype == jnp.uint32 else d
    out_shape = [jax.ShapeDtypeStruct((b, n, d_out), out_dtype)]
    out_specs = [pl.BlockSpec((None, tm, d_out), lambda bi, i: (bi, i, 0))]
    if with_router:
        e = router_t.shape[0]
        in_specs.append(pl.BlockSpec((e, d), lambda bi, i: (0, 0)))
        args.append(router_t)
        out_shape.append(jax.ShapeDtypeStruct((b, e, n), F32))
        out_specs.append(pl.BlockSpec((None, e, tm), lambda bi, i: (bi, 0, i)))
    out = pl.pallas_call(
        functools.partial(_prep_kernel, eps=eps, mod_row=mod_row, modulate=modulate,
                          with_router=with_router),
        out_shape=out_shape,
        grid=(b, n // tm),
        in_specs=in_specs,
        out_specs=out_specs,
        compiler_params=_params("arbitrary", "arbitrary"),
        name="prep",
    )(*args)
    return out if with_router else out[0]


def _mm_kernel(a_ref, w_ref, *rest, mode, rope, tiles_per_batch, mod_row):
    o_ref, wb_ref = rest[-2:]
    i = pl.program_id(1)

    @pl.when(i == 0)
    def _():
        wb_ref[...] = w_ref[...].astype(BF16)

    if mode == "resid":
        acc = jnp.dot(a_ref[...], wb_ref[...], preferred_element_type=F32)
        x_ref, gate_ref = rest[:2]
        row = i // tiles_per_batch if mod_row is None else mod_row
        o_ref[...] = x_ref[...] + gate_ref[pl.ds(row, 1), :] * acc
        return

    chunk = min(MM_ROW_CHUNK, o_ref.shape[0])
    if rope:
        even = lax.broadcasted_iota(jnp.int32, (chunk, LANES), 1) % 2 == 0
    else:
        scale = rest[0][...]
    for r in range(o_ref.shape[0] // chunk):
        rows = slice(r * chunk, (r + 1) * chunk)
        acc = jnp.dot(a_ref[rows, :], wb_ref[...], preferred_element_type=F32)
        if rope:
            cos = rest[0][rows, :]
            sin = rest[1][rows, :]
        for c in range(o_ref.shape[1] // LANES):
            xc = acc[:, c * LANES:(c + 1) * LANES]
            if rope:
                partner = jnp.where(even, pltpu.roll(xc, LANES - 1, 1), pltpu.roll(xc, 1, 1))
                xc = xc * cos + partner * sin
            else:
                xc = xc * scale
            o_ref[rows, c * LANES:(c + 1) * LANES] = xc.astype(o_ref.dtype)


def _matmul(a, w, layer, *, mode, out_dtype=BF16, tables=None, resid=None, mod=None, gate_chunk=None,
            rows_per_batch=None, mod_row=None):
    m, k = a.shape
    n = w.shape[2]
    tm = min(1024, m)
    tn = min(512, n)
    if rows_per_batch is not None:
        tm = min(tm, rows_per_batch)
    in_specs = [pl.BlockSpec((tm, k), lambda j, i: (i, 0)),
                pl.BlockSpec((None, k, tn), lambda j, i: (layer, 0, j))]
    args = [a, w]
    tiles_per_batch = 1
    rope = isinstance(tables, tuple)
    if mode == "qkv":
        nq_tiles = n // 3 // tn
        if rope:
            nblk = tables[0].shape[1] // tm
            spec = pl.BlockSpec((None, tm, LANES), lambda j, i: (j // nq_tiles, i % nblk, 0))
            in_specs += [spec, spec]
            args += list(tables)
        else:
            in_specs.append(pl.BlockSpec((None, 1, LANES), lambda j, i: (j // nq_tiles, 0, 0)))
            args.append(tables)
    else:
        tiles_per_batch = rows_per_batch // tm
        gate_off = gate_chunk * (n // tn)
        in_specs += [pl.BlockSpec((tm, tn), lambda j, i: (i, j)),
                     pl.BlockSpec((COND_ROWS, tn), lambda j, i: (0, gate_off + j))]
        args += [resid, mod]
    return pl.pallas_call(
        functools.partial(_mm_kernel, mode=mode, rope=rope, tiles_per_batch=tiles_per_batch,
                          mod_row=mod_row),
        out_shape=jax.ShapeDtypeStruct((m, n), out_dtype),
        grid=(n // tn, m // tm),
        in_specs=in_specs,
        out_specs=pl.BlockSpec((tm, tn), lambda j, i: (i, j)),
        scratch_shapes=[pltpu.VMEM((k, tn), BF16)],
        compiler_params=_params("arbitrary", "arbitrary"),
        name="matmul_" + mode,
    )(*args)


def _natten_kernel(blk_ref, q_ref, k_ref, v_ref, kc_ref, vc_ref, tab_ref, o_ref, bias_ref, *, rows):
    r = pl.program_id(2)
    nblk = pl.num_programs(2)
    ws = jnp.clip(r * NA_ROW_BLOCK - WIN_ROWS // 2, 0, rows - NA_KEY_ROWS)
    start = pl.multiple_of(ws * GRID_W, GRID_W)
    nk = NA_KEY_ROWS * GRID_W
    nt = (((1,), (1,)), ((), ()))

    @pl.when((r <= 1) | (r == nblk - 1))
    def _():
        cls = jnp.where(r == 0, 0, jnp.where(r == nblk - 1, 2, 1))
        for h in range(NA_HEADS_PER_STEP):
            for a in range(NA_ROW_BLOCK):
                for kr in range(NA_KEY_ROWS):
                    blk = blk_ref[(cls * NA_ROW_BLOCK + a) * NA_KEY_ROWS + kr]
                    bias_ref[h, a * GRID_W:(a + 1) * GRID_W, kr * GRID_W:(kr + 1) * GRID_W] = tab_ref[h, blk]

    for h in range(NA_HEADS_PER_STEP):
        sl = slice(h * HEAD_DIM, (h + 1) * HEAD_DIM)
        q = q_ref[:, sl]
        k = k_ref[pl.ds(start, nk), sl]
        v = v_ref[pl.ds(start, nk), sl]
        s = lax.dot_general(q, k, nt, preferred_element_type=F32) + bias_ref[h]
        sc = lax.dot_general(q, kc_ref[:, sl], nt, preferred_element_type=F32)
        m = jnp.maximum(jnp.max(s, axis=-1, keepdims=True), jnp.max(sc, axis=-1, keepdims=True))
        p = jnp.exp2(s - m)
        pc = jnp.exp2(sc - m)
        l = jnp.sum(p, axis=-1, keepdims=True) + jnp.sum(pc, axis=-1, keepdims=True)
        o = (jnp.dot(p.astype(BF16), v, preferred_element_type=F32)
             + jnp.dot(pc.astype(BF16), vc_ref[:, sl], preferred_element_type=F32))
        o_ref[:, sl] = (o / l).astype(o_ref.dtype)


def _natten_bias(rpb, rows):
    nblk = rows // NA_ROW_BLOCK
    kh = WIN_ROWS
    qc = np.arange(GRID_W, dtype=np.int32)
    col_start = np.clip(qc - WIN_COLS // 2, 0, GRID_W - WIN_COLS)
    col_mask = (qc[None, :] >= col_start[:, None]) & (qc[None, :] < col_start[:, None] + WIN_COLS)
    col_idx = np.clip(qc[None, :] - qc[:, None] + WIN_COLS - 1, 0, 2 * WIN_COLS - 2).astype(np.int32)
    row_idx = np.zeros((3, NA_ROW_BLOCK, NA_KEY_ROWS), np.int32)
    valid = np.zeros((3, NA_ROW_BLOCK, NA_KEY_ROWS), bool)
    for cls, blk in enumerate((0, 1, nblk - 1)):
        r0 = blk * NA_ROW_BLOCK
        ws = int(np.clip(r0 - WIN_ROWS // 2, 0, rows - NA_KEY_ROWS))
        for a in range(NA_ROW_BLOCK):
            r = r0 + a
            rs = int(np.clip(r - kh // 2, 0, rows - kh))
            for kr in range(NA_KEY_ROWS):
                key_row = ws + kr
                if rs <= key_row < rs + kh:
                    valid[cls, a, kr] = True
                    row_idx[cls, a, kr] = key_row - r + (WIN_ROWS - 1)
    for blk in range(1, nblk - 1):
        r0 = blk * NA_ROW_BLOCK
        ws = int(np.clip(r0 - WIN_ROWS // 2, 0, rows - NA_KEY_ROWS))
        for a in range(NA_ROW_BLOCK):
            rs = int(np.clip(r0 + a - kh // 2, 0, rows - kh))
            assert rs - ws == a and r0 - ws == WIN_ROWS // 2
    outside = 2 * WIN_ROWS - 1
    blk_ids = np.where(valid, row_idx, outside).astype(np.int32).reshape(-1)
    t = jnp.where(col_mask[None, None], rpb[:, :, col_idx].astype(F32) * LOG2E, NEG_INF)
    t = jnp.concatenate([t, jnp.full((rpb.shape[0], 1, GRID_W, GRID_W), NEG_INF, F32)], axis=1)
    return jnp.asarray(blk_ids), t


def _natten(qkv_l, qkv_c, bias):
    blk_ids, tab = bias
    b, n, d3 = qkv_l.shape
    d = d3 // 3
    hw = NA_HEADS_PER_STEP * HEAD_DIM
    ng = d // hw
    ctx = qkv_c.shape[1]
    rows = n // GRID_W
    nblk = rows // NA_ROW_BLOCK
    tq = NA_ROW_BLOCK * GRID_W
    nk = NA_KEY_ROWS * GRID_W

    hps = NA_HEADS_PER_STEP
    return pl.pallas_call(
        functools.partial(_natten_kernel, rows=rows),
        out_shape=jax.ShapeDtypeStruct((b, n, d), BF16),
        grid_spec=pltpu.PrefetchScalarGridSpec(
            num_scalar_prefetch=1,
            grid=(b, ng, nblk),
            in_specs=[
                pl.BlockSpec((None, tq, hw), lambda bi, h, r, ids: (bi, r, h)),
                pl.BlockSpec((None, n, hw), lambda bi, h, r, ids: (bi, 0, ng + h)),
                pl.BlockSpec((None, n, hw), lambda bi, h, r, ids: (bi, 0, 2 * ng + h)),
                pl.BlockSpec((None, ctx, hw), lambda bi, h, r, ids: (bi, 0, ng + h)),
                pl.BlockSpec((None, ctx, hw), lambda bi, h, r, ids: (bi, 0, 2 * ng + h)),
                pl.BlockSpec((hps, 2 * WIN_ROWS, GRID_W, GRID_W), lambda bi, h, r, ids: (h, 0, 0, 0)),
            ],
            out_specs=pl.BlockSpec((None, tq, hw), lambda bi, h, r, ids: (bi, r, h)),
            scratch_shapes=[pltpu.VMEM((hps, tq, nk), F32)],
        ),
        compiler_params=_params("arbitrary", "arbitrary", "arbitrary"),
        name="natten",
    )(blk_ids, qkv_l, qkv_l, qkv_l, qkv_c, qkv_c, tab)


def _ctx_attn_kernel(q_ref, k_ref, v_ref, o_ref):
    s = lax.dot_general(q_ref[...], k_ref[...], (((1,), (1,)), ((), ())), preferred_element_type=F32)
    p = jnp.exp2(s - jnp.max(s, axis=-1, keepdims=True))
    l = jnp.sum(p, axis=-1, keepdims=True)
    o = jnp.dot(p.astype(BF16), v_ref[...], preferred_element_type=F32)
    o_ref[...] = (o / l).astype(o_ref.dtype)


def _ctx_attn(qkv_c):
    b, ctx, d3 = qkv_c.shape
    d = d3 // 3
    nh = d // HEAD_DIM
    return pl.pallas_call(
        _ctx_attn_kernel,
        out_shape=jax.ShapeDtypeStruct((b, ctx, d), BF16),
        grid=(b, nh),
        in_specs=[
            pl.BlockSpec((None, ctx, HEAD_DIM), lambda bi, h: (bi, 0, h)),
            pl.BlockSpec((None, ctx, HEAD_DIM), lambda bi, h: (bi, 0, nh + h)),
            pl.BlockSpec((None, ctx, HEAD_DIM), lambda bi, h: (bi, 0, 2 * nh + h)),
        ],
        out_specs=pl.BlockSpec((None, ctx, HEAD_DIM), lambda bi, h: (bi, 0, h)),
        compiler_params=_params("arbitrary", "arbitrary"),
        name="ctx_attn",
    )(qkv_c, qkv_c, qkv_c)


def _lane_tile(x, width):
    return jnp.concatenate([x] * (width // LANES), axis=1)


def _diff_kernel(q_ref, k_ref, v_ref, kc_ref, vc_ref, lq1_ref, lk1_ref, lq2_ref, lk2_ref, g_ref,
                 o_ref, m1, m2, l1, l2, a1, a2, *, tk, unroll, lambda_init):
    n = k_ref.shape[0]
    nt = (((1,), (1,)), ((), ()))
    ms, ls, accs = (m1, m2), (l1, l2), (a1, a2)
    for c in range(2):
        ms[c][...] = jnp.full(ms[c].shape, NEG_INF, F32)
        ls[c][...] = jnp.zeros(ls[c].shape, F32)
        accs[c][...] = jnp.zeros(accs[c].shape, F32)

    def step(k_blk, v_blk):
        for c in range(2):
            q = q_ref[:, c * HEAD_DIM:(c + 1) * HEAD_DIM]
            s = lax.dot_general(q, k_blk[:, c * HEAD_DIM:(c + 1) * HEAD_DIM], nt,
                                preferred_element_type=F32)
            m_prev = ms[c][...]
            m_new = jnp.maximum(m_prev, jnp.max(s, axis=-1, keepdims=True))
            alpha = jnp.exp2(m_prev - m_new)
            p = jnp.exp2(s - _lane_tile(m_new, s.shape[1]))
            ls[c][...] = alpha * ls[c][...] + jnp.sum(p, axis=-1, keepdims=True)
            pv = jnp.dot(p.astype(BF16), v_blk, preferred_element_type=F32)
            accs[c][...] = _lane_tile(alpha, pv.shape[1]) * accs[c][...] + pv
            ms[c][...] = m_new

    def body(j, carry):
        st = pl.multiple_of(j * tk, tk)
        step(k_ref[pl.ds(st, tk), :], v_ref[pl.ds(st, tk), :])
        return carry

    lax.fori_loop(0, n // tk, body, 0, unroll=unroll)
    step(kc_ref[...], vc_ref[...])

    lam = (jnp.exp(jnp.sum(lq1_ref[...] * lk1_ref[...], axis=-1, keepdims=True))
           - jnp.exp(jnp.sum(lq2_ref[...] * lk2_ref[...], axis=-1, keepdims=True)) + lambda_init)
    w = a1.shape[1]
    o = a1[...] / _lane_tile(l1[...], w) - lam * (a2[...] / _lane_tile(l2[...], w))
    o = o * lax.rsqrt(jnp.mean(o * o, axis=-1, keepdims=True) + SUBLN_EPS) * g_ref[...]
    o_ref[...] = (o * (1.0 - lambda_init)).astype(o_ref.dtype)


def _diff_attn(qkv_l, qkv_c, lq1, lk1, lq2, lk2, subln_g, lambda_init):
    b, n, d3 = qkv_l.shape
    d = d3 // 3
    hw = 2 * HEAD_DIM
    nh = d // hw
    ctx = qkv_c.shape[1]
    tq = min(DA_Q_TILE, n)
    tk = min(DA_K_TILE, n)
    vec = lambda a: a.reshape(1, -1).astype(F32)
    small = pl.BlockSpec((1, HEAD_DIM), lambda bi, h, i: (0, 0))
    return pl.pallas_call(
        functools.partial(_diff_kernel, tk=tk, unroll=min(DA_UNROLL, n // tk), lambda_init=lambda_init),
        out_shape=jax.ShapeDtypeStruct((b, n, d), BF16),
        grid=(b, nh, n // tq),
        in_specs=[
            pl.BlockSpec((None, tq, hw), lambda bi, h, i: (bi, i, h)),
            pl.BlockSpec((None, n, hw), lambda bi, h, i: (bi, 0, nh + h)),
            pl.BlockSpec((None, n, hw), lambda bi, h, i: (bi, 0, 2 * nh + h)),
            pl.BlockSpec((None, ctx, hw), lambda bi, h, i: (bi, 0, nh + h)),
            pl.BlockSpec((None, ctx, hw), lambda bi, h, i: (bi, 0, 2 * nh + h)),
            small, small, small, small,
            pl.BlockSpec((1, hw), lambda bi, h, i: (0, 0)),
        ],
        out_specs=pl.BlockSpec((None, tq, hw), lambda bi, h, i: (bi, i, h)),
        scratch_shapes=[pltpu.VMEM((tq, LANES), F32)] * 4 + [pltpu.VMEM((tq, hw), F32)] * 2,
        compiler_params=_params("arbitrary", "arbitrary", "arbitrary"),
        name="diff_attn",
    )(qkv_l, qkv_l, qkv_l, qkv_c, qkv_c, vec(lq1), vec(lk1), vec(lq2), vec(lk2), vec(subln_g))


def _ffn_kernel(idx_ref, h_ref, gate_ref, mod_ref, wg_ref, wu_ref, wd_ref, o_ref,
                gbuf, xlo, xhi, a_ref, gsem, *, nf, tf, tc, tiles_per_batch, mod_row):
    m = pl.program_id(1)
    s = pl.program_id(2)
    nm = pl.num_programs(1)
    tile = pl.program_id(0) * nm + m
    ntiles = pl.num_programs(0) * nm

    def start_gather(tl):
        base = tl * tc

        def body(i, carry):
            src = h_ref.at[pl.ds(idx_ref[base + i], 1), :]
            pltpu.make_async_copy(src, gbuf.at[pl.ds(i, 1), :], gsem).start()
            return carry

        lax.fori_loop(0, tc, body, 0, unroll=8)

    @pl.when((tile == 0) & (s == 0))
    def _():
        start_gather(0)

    @pl.when(s == 0)
    def _():
        pltpu.make_async_copy(h_ref.at[pl.ds(0, tc), :], gbuf, gsem).wait()
        w = gbuf[...]
        xlo[...] = pltpu.bitcast(w << 16, F32).astype(BF16)
        xhi[...] = pltpu.bitcast(w & jnp.uint32(0xFFFF0000), F32).astype(BF16)

    @pl.when((s == nf) & (tile + 1 < ntiles))
    def _():
        start_gather(tile + 1)

    @pl.when(s < nf)
    def _():
        half = xlo.shape[1]
        wg = wg_ref[...].astype(BF16)
        wu = wu_ref[...].astype(BF16)
        g = (jnp.dot(xlo[...], wg[:half], preferred_element_type=F32)
             + jnp.dot(xhi[...], wg[half:], preferred_element_type=F32))
        u = (jnp.dot(xlo[...], wu[:half], preferred_element_type=F32)
             + jnp.dot(xhi[...], wu[half:], preferred_element_type=F32))
        a = (g * (1.0 / (1.0 + jnp.exp(-g))) * u).astype(BF16)
        for f in range(nf):
            @pl.when(s == f)
            def _():
                a_ref[:, f * tf:(f + 1) * tf] = a

    @pl.when(s >= nf)
    def _():
        y = jnp.dot(a_ref[...], wd_ref[...].astype(BF16), preferred_element_type=F32)
        row = m // tiles_per_batch if mod_row is None else mod_row
        o_ref[...] = y * gate_ref[...] * mod_ref[pl.ds(row, 1), :]


def _ffn(h_packed, idx_flat, gate, mod, w_gate, w_up, w_down, layer, *, cap, mod_row=None):
    n, half = h_packed.shape
    d = 2 * half
    e, r, _ = gate.shape
    ff = w_gate.shape[3]
    tc = min(1024, r if mod_row is not None else cap)
    tf = min(256, ff)
    tn = min(1024, d)
    nf = ff // tf
    nd = d // tn
    assert r % tc == 0 and tc <= n and (mod_row is not None or cap % tc == 0)
    tiles_per_batch = max(cap // tc, 1)
    mod_off = (N_MOD - 1) * nd
    return pl.pallas_call(
        functools.partial(_ffn_kernel, nf=nf, tf=tf, tc=tc, tiles_per_batch=tiles_per_batch,
                          mod_row=mod_row),
        out_shape=jax.ShapeDtypeStruct((e, r, d), F32),
        grid_spec=pltpu.PrefetchScalarGridSpec(
            num_scalar_prefetch=1,
            grid=(e, r // tc, nf + nd),
            in_specs=[
                pl.BlockSpec(memory_space=pl.ANY),
                pl.BlockSpec((None, tc, 1), lambda ei, mi, s, idx: (ei, mi, 0)),
                pl.BlockSpec((COND_ROWS, tn), lambda ei, mi, s, idx: (0, mod_off + jnp.maximum(s - nf, 0))),
                pl.BlockSpec((None, None, d, tf),
                             lambda ei, mi, s, idx: (layer, ei, 0, jnp.minimum(s, nf - 1))),
                pl.BlockSpec((None, None, d, tf),
                             lambda ei, mi, s, idx: (layer, ei, 0, jnp.minimum(s, nf - 1))),
                pl.BlockSpec((None, None, ff, tn),
                             lambda ei, mi, s, idx: (layer, ei, 0, jnp.maximum(s - nf, 0))),
            ],
            out_specs=pl.BlockSpec((None, tc, tn), lambda ei, mi, s, idx: (ei, mi, jnp.maximum(s - nf, 0))),
            scratch_shapes=[pltpu.VMEM((tc, half), jnp.uint32), pltpu.VMEM((tc, half), BF16),
                            pltpu.VMEM((tc, half), BF16), pltpu.VMEM((tc, ff), BF16),
                            pltpu.SemaphoreType.DMA(())],
        ),
        compiler_params=pltpu.CompilerParams(dimension_semantics=("arbitrary",) * 3,
                                             vmem_limit_bytes=V7X_VMEM_LIMIT_BYTES,
                                             disable_bounds_checks=True),
        name="expert_ffn",
    )(idx_flat, h_packed, gate, mod, w_gate, w_up, w_down)


def _combine_kernel(idx_ref, y_ref, x_in_ref, x_ref, buf, gsem, ssem, *, tc):
    del x_in_ref
    t = pl.program_id(0)
    nt = pl.num_programs(0)
    slot = t % 2

    def start_all(chunk, sl, scatter):
        base = chunk * tc

        def body(i, carry):
            hbm = x_ref.at[pl.ds(idx_ref[base + i], 1), :]
            vm = buf.at[sl, pl.ds(i, 1), :]
            if scatter:
                pltpu.make_async_copy(vm, hbm, ssem.at[sl]).start()
            else:
                pltpu.make_async_copy(hbm, vm, gsem.at[sl]).start()
            return carry

        lax.fori_loop(0, tc, body, 0, unroll=8)

    def wait_all(sl, scatter):
        whole = x_ref.at[pl.ds(0, tc), :]
        if scatter:
            pltpu.make_async_copy(buf.at[sl], whole, ssem.at[sl]).wait()
        else:
            pltpu.make_async_copy(whole, buf.at[sl], gsem.at[sl]).wait()

    @pl.when(t == 0)
    def _():
        start_all(0, 0, False)

    @pl.when(t > 0)
    def _():
        wait_all(1 - slot, True)

    @pl.when(t + 1 < nt)
    def _():
        start_all(t + 1, 1 - slot, False)

    wait_all(slot, False)
    buf[slot] = buf[slot] + y_ref[...]
    start_all(t, slot, True)

    @pl.when(t == nt - 1)
    def _():
        wait_all(slot, True)


def _combine(x, y, idx_flat, cap):
    d = x.shape[1]
    e, r, _ = y.shape
    tc = min(512, cap)
    assert cap % tc == 0 and (r // cap >= 2 or cap // tc >= 2)
    nchunk = r // tc
    return pl.pallas_call(
        functools.partial(_combine_kernel, tc=tc),
        out_shape=jax.ShapeDtypeStruct(x.shape, F32),
        grid_spec=pltpu.PrefetchScalarGridSpec(
            num_scalar_prefetch=1,
            grid=(e * nchunk,),
            in_specs=[
                pl.BlockSpec((None, tc, d), lambda t, idx: (t // nchunk, t % nchunk, 0)),
                pl.BlockSpec(memory_space=pl.ANY),
            ],
            out_specs=pl.BlockSpec(memory_space=pl.ANY),
            scratch_shapes=[pltpu.VMEM((2, tc, d), F32), pltpu.SemaphoreType.DMA((2,)),
                            pltpu.SemaphoreType.DMA((2,))],
        ),
        input_output_aliases={2: 0},
        compiler_params=pltpu.CompilerParams(dimension_semantics=("arbitrary",),
                                             vmem_limit_bytes=V7X_VMEM_LIMIT_BYTES,
                                             disable_bounds_checks=True),
        name="moe_combine",
    )(idx_flat, y, x)


def _ec_moe(x, h, aff, mod, w_gate, w_up, w_down, layer, mod_row=None):
    b, n, d = x.shape
    e = aff.shape[1]
    cap = EC_CAPACITY_FACTOR * n // e
    gate, idx = lax.top_k(aff, cap)
    rows = idx + (jnp.arange(b, dtype=jnp.int32) * n)[:, None, None]
    idx_flat = rows.transpose(1, 0, 2).reshape(-1)
    gate_t = gate.transpose(1, 0, 2).reshape(e, b * cap, 1)
    y = _ffn(h.reshape(b * n, d // 2), idx_flat, gate_t, mod, w_gate, w_up, w_down, layer, cap=cap,
             mod_row=mod_row)
    return _combine(x.reshape(b * n, d), y, idx_flat, cap).reshape(b, n, d)


def _rope_tables(n):
    t = np.arange(n, dtype=np.int32)
    row = (t // GRID_W).astype(np.float32)
    col = (t % GRID_W).astype(np.float32)
    pairs = HEAD_DIM // 4
    freq = jnp.asarray(ROPE_BASE, F32) ** (-jnp.arange(pairs, dtype=F32) / pairs)
    ang = jnp.concatenate([jnp.asarray(row)[:, None] * freq, jnp.asarray(col)[:, None] * freq], axis=-1)
    cos, sin = jnp.cos(ang), jnp.sin(ang)
    cos = jnp.stack([cos, cos], axis=-1).reshape(n, HEAD_DIM)
    sin = jnp.stack([-sin, sin], axis=-1).reshape(n, HEAD_DIM)
    return (jnp.stack([cos * Q_SCALE, cos, jnp.ones_like(cos)]),
            jnp.stack([sin * Q_SCALE, sin, jnp.zeros_like(sin)]))


def _scale_table():
    return jnp.broadcast_to(jnp.asarray([Q_SCALE, 1.0, 1.0], F32)[:, None, None], (3, 1, LANES))


def kernel(x, c, ctx, c_ctx, ada_w, ada_b, norm1_g, norm2_g, na_w_qkv, na_w_o, na_rpb, da_w_qkv, da_w_o, da_lambda_q1, da_lambda_k1, da_lambda_q2, da_lambda_k2, da_subln_g, moe_w_router, moe_w_gate, moe_w_up, moe_w_down, final_g):
    b, n, d = x.shape
    n_ctx = ctx.shape[1]
    depth = ada_w.shape[0]
    rows = n // GRID_W
    assert n % GRID_W == 0 and rows % NA_ROW_BLOCK == 0 and rows // NA_ROW_BLOCK >= 3
    assert b + 1 <= COND_ROWS

    cond = jnp.zeros((COND_ROWS, d), F32).at[:b].set(c).at[b].set(c_ctx)
    ada_b3 = ada_b.reshape(depth, 1, N_MOD * d)
    rope = _rope_tables(n)
    plain = _scale_table()

    for i in range(depth):
        last = i == depth - 1
        j = i // N_MIXERS
        mod = _adaln(cond, ada_w, ada_b3, i)
        h_l = _prep(x, norm1_g[i], mod, shift_chunk=0, scale_chunk=1).reshape(b * n, d)
        h_c = _prep(ctx, norm1_g[i], mod, shift_chunk=0, scale_chunk=1, mod_row=b).reshape(b * n_ctx, d)
        if i % N_MIXERS == 0:
            w_o = na_w_o
            qkv_l = _matmul(h_l, na_w_qkv, j, mode="qkv", tables=plain).reshape(b, n, 3 * d)
            qkv_c = _matmul(h_c, na_w_qkv, j, mode="qkv", tables=plain).reshape(b, n_ctx, 3 * d)
            o_l = _natten(qkv_l, qkv_c, _natten_bias(na_rpb[j], rows))
            o_c = None if last else _ctx_attn(qkv_c)
        else:
            w_o = da_w_o
            qkv_l = _matmul(h_l, da_w_qkv, j, mode="qkv", tables=rope, rows_per_batch=n).reshape(b, n, 3 * d)
            qkv_c = _matmul(h_c, da_w_qkv, j, mode="qkv", tables=plain).reshape(b, n_ctx, 3 * d)
            o_l = _diff_attn(qkv_l, qkv_c, da_lambda_q1[j], da_lambda_k1[j], da_lambda_q2[j],
                             da_lambda_k2[j], da_subln_g[j], _lambda_init(i))
            assert last, "context update after a differential layer is not needed at this depth"
            o_c = None
        x = _matmul(o_l.reshape(b * n, d), w_o, j, mode="resid", out_dtype=F32, resid=x.reshape(b * n, d),
                    mod=mod, gate_chunk=2, rows_per_batch=n).reshape(b, n, d)
        router_t = moe_w_router[i].T
        h2, aff = _prep(x, norm2_g[i], mod, shift_chunk=3, scale_chunk=4, router_t=router_t,
                        out_dtype=jnp.uint32)
        x = _ec_moe(x, h2, aff, mod, moe_w_gate, moe_w_up, moe_w_down, i)
        if not last:
            ctx = _matmul(o_c.reshape(b * n_ctx, d), w_o, j, mode="resid", out_dtype=F32,
                          resid=ctx.reshape(b * n_ctx, d), mod=mod, gate_chunk=2,
                          rows_per_batch=n_ctx, mod_row=b).reshape(b, n_ctx, d)
            h2c, affc = _prep(ctx, norm2_g[i], mod, shift_chunk=3, scale_chunk=4, mod_row=b,
                              router_t=router_t, out_dtype=jnp.uint32)
            ctx = _ec_moe(ctx, h2c, affc, mod, moe_w_gate, moe_w_up, moe_w_down, i, mod_row=b)
    return _prep(x, final_g, None, out_dtype=F32)
```

```python
import functools
import math

import numpy as np
import jax
import jax.numpy as jnp
from jax import lax
from jax.experimental import pallas as pl
from jax.experimental.pallas import tpu as pltpu

F32 = jnp.float32
BF16 = jnp.bfloat16

GRID_W = 64
WIN_ROWS = 8
WIN_COLS = 16
HEAD_DIM = 128
N_MIXERS = 2
N_MOD = 6
ROPE_BASE = 10000.0
EC_CAPACITY_FACTOR = 2
NORM_EPS = 1e-6
SUBLN_EPS = 1e-5
NEG_INF = -1e30
LOG2E = math.log2(math.e)
Q_SCALE = HEAD_DIM ** -0.5 * LOG2E

V7X_VMEM_LIMIT_BYTES = 56 * 1024 * 1024
LANES = 128
COND_ROWS = 8
NA_ROW_BLOCK = 4
NA_KEY_ROWS = NA_ROW_BLOCK + WIN_ROWS - 1
NA_HEADS_PER_STEP = 4
DA_Q_TILE = 1024
DA_K_TILE = 512
DA_UNROLL = 8
MM_ROW_CHUNK = 256


def _params(*sem):
    return pltpu.CompilerParams(dimension_semantics=sem, vmem_limit_bytes=V7X_VMEM_LIMIT_BYTES)


def _lambda_init(layer_idx):
    return 0.8 - 0.6 * math.exp(-0.3 * layer_idx)


def _adaln_kernel(s_ref, w_ref, b_ref, o_ref):
    s = s_ref[...]
    s = s * (1.0 / (1.0 + jnp.exp(-s)))
    acc = jnp.dot(s.astype(BF16), w_ref[...].astype(BF16), preferred_element_type=F32)
    o_ref[...] = acc + b_ref[...]


def _adaln(cond, ada_w, ada_b3, layer):
    d = cond.shape[1]
    n6 = ada_w.shape[2]
    tn = min(512, n6)
    return pl.pallas_call(
        _adaln_kernel,
        out_shape=jax.ShapeDtypeStruct((COND_ROWS, n6), F32),
        grid=(n6 // tn,),
        in_specs=[
            pl.BlockSpec((COND_ROWS, d), lambda j: (0, 0)),
            pl.BlockSpec((None, d, tn), lambda j: (layer, 0, j)),
            pl.BlockSpec((None, 1, tn), lambda j: (layer, 0, j)),
        ],
        out_specs=pl.BlockSpec((COND_ROWS, tn), lambda j: (0, j)),
        compiler_params=_params("arbitrary"),
        name="adaln",
    )(cond, ada_w, ada_b3)


def _prep_kernel(x_ref, g_ref, *rest, eps, mod_row, modulate, with_router):
    rest = list(rest)
    if modulate:
        sh_ref, sc_ref = rest[:2]
        rest = rest[2:]
    if with_router:
        wr_ref, o_ref, aff_ref = rest
    else:
        (o_ref,) = rest
    x = x_ref[...]
    h = x * lax.rsqrt(jnp.mean(x * x, axis=-1, keepdims=True) + eps) * g_ref[...]
    if modulate:
        row = pl.program_id(0) if mod_row is None else mod_row
        h = h * (1.0 + sc_ref[pl.ds(row, 1), :]) + sh_ref[pl.ds(row, 1), :]
    if o_ref.dtype == jnp.uint32:
        bits = pltpu.bitcast(h.astype(BF16).astype(F32), jnp.uint32)
        half = o_ref.shape[1]
        o_ref[...] = (bits[:, :half] >> 16) | bits[:, half:]
    else:
        o_ref[...] = h.astype(o_ref.dtype)
    if with_router:
        logits = lax.dot_general(wr_ref[...], h, (((1,), (1,)), ((), ())),
                                 precision=lax.Precision.HIGHEST, preferred_element_type=F32)
        e = jnp.exp(logits - jnp.max(logits, axis=0, keepdims=True))
        aff_ref[...] = e / jnp.sum(e, axis=0, keepdims=True)


def _prep(x, g, mod, *, shift_chunk=None, scale_chunk=None, mod_row=None, out_dtype=BF16,
          router_t=None, eps=NORM_EPS):
    b, n, d = x.shape
    tm = min(256, n)
    modulate = shift_chunk is not None
    with_router = router_t is not None
    in_specs = [pl.BlockSpec((None, tm, d), lambda bi, i: (bi, i, 0)),
                pl.BlockSpec((1, d), lambda bi, i: (0, 0))]
    args = [x, g.reshape(1, d)]
    if modulate:
        in_specs += [pl.BlockSpec((COND_ROWS, d), lambda bi, i: (0, shift_chunk)),
                     pl.BlockSpec((COND_ROWS, d), lambda bi, i: (0, scale_chunk))]
        args += [mod, mod]
    d_out = d // 2 if out_dtype == jnp.uint32 else d
    out_shape = [jax.ShapeDtypeStruct((b, n, d_out), out_dtype)]
    out_specs = [pl.BlockSpec((None, tm, d_out), lambda bi, i: (bi, i, 0))]
    if with_router:
        e = router_t.shape[0]
        in_specs.append(pl.BlockSpec((e, d), lambda bi, i: (0, 0)))
        args.append(router_t)
        out_shape.append(jax.ShapeDtypeStruct((b, e, n), F32))
        out_specs.append(pl.BlockSpec((None, e, tm), lambda bi, i: (bi, 0, i)))
    out = pl.pallas_call(
        functools.partial(_prep_kernel, eps=eps, mod_row=mod_row, modulate=modulate,
                          with_router=with_router),
        out_shape=out_shape,
        grid=(b, n // tm),
        in_specs=in_specs,
        out_specs=out_specs,
        compiler_params=_params("arbitrary", "arbitrary"),
        name="prep",
    )(*args)
    return out if with_router else out[0]


def _mm_kernel(a_ref, w_ref, *rest, mode, rope, tiles_per_batch, mod_row):
    o_ref, wb_ref = rest[-2:]
    i = pl.program_id(1)

    @pl.when(i == 0)
    def _():
        wb_ref[...] = w_ref[...].astype(BF16)

    if mode == "resid":
        acc = jnp.dot(a_ref[...], wb_ref[...], preferred_element_type=F32)
        x_ref, gate_ref = rest[:2]
        row = i // tiles_per_batch if mod_row is None else mod_row
        o_ref[...] = x_ref[...] + gate_ref[pl.ds(row, 1), :] * acc
        return

    chunk = min(MM_ROW_CHUNK, o_ref.shape[0])
    if rope:
        even = lax.broadcasted_iota(jnp.int32, (chunk, LANES), 1) % 2 == 0
    else:
        scale = rest[0][...]
    for r in range(o_ref.shape[0] // chunk):
        rows = slice(r * chunk, (r + 1) * chunk)
        acc = jnp.dot(a_ref[rows, :], wb_ref[...], preferred_element_type=F32)
        if rope:
            cos = rest[0][rows, :]
            sin = rest[1][rows, :]
        for c in range(o_ref.shape[1] // LANES):
            xc = acc[:, c * LANES:(c + 1) * LANES]
            if rope:
                partner = jnp.where(even, pltpu.roll(xc, LANES - 1, 1), pltpu.roll(xc, 1, 1))
                xc = xc * cos + partner * sin
            else:
                xc = xc * scale
            o_ref[rows, c * LANES:(c + 1) * LANES] = xc.astype(o_ref.dtype)


def _matmul(a, w, layer, *, mode, out_dtype=BF16, tables=None, resid=None, mod=None, gate_chunk=None,
            rows_per_batch=None, mod_row=None):
    m, k = a.shape
    n = w.shape[2]
    tm = min(1024, m)
    tn = min(512, n)
    if rows_per_batch is not None:
        tm = min(tm, rows_per_batch)
    in_specs = [pl.BlockSpec((tm, k), lambda j, i: (i, 0)),
                pl.BlockSpec((None, k, tn), lambda j, i: (layer, 0, j))]
    args = [a, w]
    tiles_per_batch = 1
    rope = isinstance(tables, tuple)
    if mode == "qkv":
        nq_tiles = n // 3 // tn
        if rope:
            nblk = tables[0].shape[1] // tm
            spec = pl.BlockSpec((None, tm, LANES), lambda j, i: (j // nq_tiles, i % nblk, 0))
            in_specs += [spec, spec]
            args += list(tables)
        else:
            in_specs.append(pl.BlockSpec((None, 1, LANES), lambda j, i: (j // nq_tiles, 0, 0)))
            args.append(tables)
    else:
        tiles_per_batch = rows_per_batch // tm
        gate_off = gate_chunk * (n // tn)
        in_specs += [pl.BlockSpec((tm, tn), lambda j, i: (i, j)),
                     pl.BlockSpec((COND_ROWS, tn), lambda j, i: (0, gate_off + j))]
        args += [resid, mod]
    return pl.pallas_call(
        functools.partial(_mm_kernel, mode=mode, rope=rope, tiles_per_batch=tiles_per_batch,
                          mod_row=mod_row),
        out_shape=jax.ShapeDtypeStruct((m, n), out_dtype),
        grid=(n // tn, m // tm),
        in_specs=in_specs,
        out_specs=pl.BlockSpec((tm, tn), lambda j, i: (i, j)),
        scratch_shapes=[pltpu.VMEM((k, tn), BF16)],
        compiler_params=_params("arbitrary", "arbitrary"),
        name="matmul_" + mode,
    )(*args)


def _natten_kernel(blk_ref, q_ref, k_ref, v_ref, kc_ref, vc_ref, tab_ref, o_ref, bias_ref, *, rows):
    r = pl.program_id(2)
    nblk = pl.num_programs(2)
    ws = jnp.clip(r * NA_ROW_BLOCK - WIN_ROWS // 2, 0, rows - NA_KEY_ROWS)
    start = pl.multiple_of(ws * GRID_W, GRID_W)
    nk = NA_KEY_ROWS * GRID_W
    nt = (((1,), (1,)), ((), ()))

    @pl.when((r <= 1) | (r == nblk - 1))
    def _():
        cls = jnp.where(r == 0, 0, jnp.where(r == nblk - 1, 2, 1))
        for h in range(NA_HEADS_PER_STEP):
            for a in range(NA_ROW_BLOCK):
                for kr in range(NA_KEY_ROWS):
                    blk = blk_ref[(cls * NA_ROW_BLOCK + a) * NA_KEY_ROWS + kr]
                    bias_ref[h, a * GRID_W:(a + 1) * GRID_W, kr * GRID_W:(kr + 1) * GRID_W] = tab_ref[h, blk]

    for h in range(NA_HEADS_PER_STEP):
        sl = slice(h * HEAD_DIM, (h + 1) * HEAD_DIM)
        q = q_ref[:, sl]
        k = k_ref[pl.ds(start, nk), sl]
        v = v_ref[pl.ds(start, nk), sl]
        s = lax.dot_general(q, k, nt, preferred_element_type=F32) + bias_ref[h]
        sc = lax.dot_general(q, kc_ref[:, sl], nt, preferred_element_type=F32)
        m = jnp.maximum(jnp.max(s, axis=-1, keepdims=True), jnp.max(sc, axis=-1, keepdims=True))
        p = jnp.exp2(s - m)
        pc = jnp.exp2(sc - m)
        vc = vc_ref[:, sl]
        ones = jnp.ones((nk, HEAD_DIM), BF16)
        oa = (jnp.dot(p.astype(BF16), jnp.concatenate([v, ones], axis=1), preferred_element_type=F32)
              + jnp.dot(pc.astype(BF16), jnp.concatenate([vc, ones[:vc.shape[0]]], axis=1),
                        preferred_element_type=F32))
        o_ref[:, sl] = (oa[:, :HEAD_DIM] / oa[:, HEAD_DIM:]).astype(o_ref.dtype)


def _natten_bias(rpb, rows):
    nblk = rows // NA_ROW_BLOCK
    kh = WIN_ROWS
    qc = np.arange(GRID_W, dtype=np.int32)
    col_start = np.clip(qc - WIN_COLS // 2, 0, GRID_W - WIN_COLS)
    col_mask = (qc[None, :] >= col_start[:, None]) & (qc[None, :] < col_start[:, None] + WIN_COLS)
    col_idx = np.clip(qc[None, :] - qc[:, None] + WIN_COLS - 1, 0, 2 * WIN_COLS - 2).astype(np.int32)
    row_idx = np.zeros((3, NA_ROW_BLOCK, NA_KEY_ROWS), np.int32)
    valid = np.zeros((3, NA_ROW_BLOCK, NA_KEY_ROWS), bool)
    for cls, blk in enumerate((0, 1, nblk - 1)):
        r0 = blk * NA_ROW_BLOCK
        ws = int(np.clip(r0 - WIN_ROWS // 2, 0, rows - NA_KEY_ROWS))
        for a in range(NA_ROW_BLOCK):
            r = r0 + a
            rs = int(np.clip(r - kh // 2, 0, rows - kh))
            for kr in range(NA_KEY_ROWS):
                key_row = ws + kr
                if rs <= key_row < rs + kh:
                    valid[cls, a, kr] = True
                    row_idx[cls, a, kr] = key_row - r + (WIN_ROWS - 1)
    for blk in range(1, nblk - 1):
        r0 = blk * NA_ROW_BLOCK
        ws = int(np.clip(r0 - WIN_ROWS // 2, 0, rows - NA_KEY_ROWS))
        for a in range(NA_ROW_BLOCK):
            rs = int(np.clip(r0 + a - kh // 2, 0, rows - kh))
            assert rs - ws == a and r0 - ws == WIN_ROWS // 2
    outside = 2 * WIN_ROWS - 1
    blk_ids = np.where(valid, row_idx, outside).astype(np.int32).reshape(-1)
    t = jnp.where(col_mask[None, None], rpb[:, :, col_idx].astype(F32) * LOG2E, NEG_INF)
    t = jnp.concatenate([t, jnp.full((rpb.shape[0], 1, GRID_W, GRID_W), NEG_INF, F32)], axis=1)
    return jnp.asarray(blk_ids), t


def _natten(qkv_l, qkv_c, bias):
    blk_ids, tab = bias
    b, n, d3 = qkv_l.shape
    d = d3 // 3
    hw = NA_HEADS_PER_STEP * HEAD_DIM
    ng = d // hw
    ctx = qkv_c.shape[1]
    rows = n // GRID_W
    nblk = rows // NA_ROW_BLOCK
    tq = NA_ROW_BLOCK * GRID_W
    nk = NA_KEY_ROWS * GRID_W

    hps = NA_HEADS_PER_STEP
    return pl.pallas_call(
        functools.partial(_natten_kernel, rows=rows),
        out_shape=jax.ShapeDtypeStruct((b, n, d), BF16),
        grid_spec=pltpu.PrefetchScalarGridSpec(
            num_scalar_prefetch=1,
            grid=(b, ng, nblk),
            in_specs=[
                pl.BlockSpec((None, tq, hw), lambda bi, h, r, ids: (bi, r, h)),
                pl.BlockSpec((None, n, hw), lambda bi, h, r, ids: (bi, 0, ng + h)),
                pl.BlockSpec((None, n, hw), lambda bi, h, r, ids: (bi, 0, 2 * ng + h)),
                pl.BlockSpec((None, ctx, hw), lambda bi, h, r, ids: (bi, 0, ng + h)),
                pl.BlockSpec((None, ctx, hw), lambda bi, h, r, ids: (bi, 0, 2 * ng + h)),
                pl.BlockSpec((hps, 2 * WIN_ROWS, GRID_W, GRID_W), lambda bi, h, r, ids: (h, 0, 0, 0)),
            ],
            out_specs=pl.BlockSpec((None, tq, hw), lambda bi, h, r, ids: (bi, r, h)),
            scratch_shapes=[pltpu.VMEM((hps, tq, nk), F32)],
        ),
        compiler_params=_params("arbitrary", "arbitrary", "arbitrary"),
        name="natten",
    )(blk_ids, qkv_l, qkv_l, qkv_l, qkv_c, qkv_c, tab)


def _ctx_attn_kernel(q_ref, k_ref, v_ref, o_ref):
    s = lax.dot_general(q_ref[...], k_ref[...], (((1,), (1,)), ((), ())), preferred_element_type=F32)
    p = jnp.exp2(s - jnp.max(s, axis=-1, keepdims=True))
    l = jnp.sum(p, axis=-1, keepdims=True)
    o = jnp.dot(p.astype(BF16), v_ref[...], preferred_element_type=F32)
    o_ref[...] = (o / l).astype(o_ref.dtype)


def _ctx_attn(qkv_c):
    b, ctx, d3 = qkv_c.shape
    d = d3 // 3
    nh = d // HEAD_DIM
    return pl.pallas_call(
        _ctx_attn_kernel,
        out_shape=jax.ShapeDtypeStruct((b, ctx, d), BF16),
        grid=(b, nh),
        in_specs=[
            pl.BlockSpec((None, ctx, HEAD_DIM), lambda bi, h: (bi, 0, h)),
            pl.BlockSpec((None, ctx, HEAD_DIM), lambda bi, h: (bi, 0, nh + h)),
            pl.BlockSpec((None, ctx, HEAD_DIM), lambda bi, h: (bi, 0, 2 * nh + h)),
        ],
        out_specs=pl.BlockSpec((None, ctx, HEAD_DIM), lambda bi, h: (bi, 0, h)),
        compiler_params=_params("arbitrary", "arbitrary"),
        name="ctx_attn",
    )(qkv_c, qkv_c, qkv_c)


def _lane_tile(x, width):
    return jnp.concatenate([x] * (width // LANES), axis=1)


def _diff_kernel(q_ref, k_ref, v_ref, kc_ref, vc_ref, lq1_ref, lk1_ref, lq2_ref, lk2_ref, g_ref,
                 o_ref, m1, m2, l1, l2, a1, a2, *, tk, unroll, lambda_init):
    n = k_ref.shape[0]
    nt = (((1,), (1,)), ((), ()))
    ms, ls, accs = (m1, m2), (l1, l2), (a1, a2)
    for c in range(2):
        ms[c][...] = jnp.full(ms[c].shape, NEG_INF, F32)
        ls[c][...] = jnp.zeros(ls[c].shape, F32)
        accs[c][...] = jnp.zeros(accs[c].shape, F32)

    def step(k_blk, v_blk):
        for c in range(2):
            q = q_ref[:, c * HEAD_DIM:(c + 1) * HEAD_DIM]
            s = lax.dot_general(q, k_blk[:, c * HEAD_DIM:(c + 1) * HEAD_DIM], nt,
                                preferred_element_type=F32)
            m_prev = ms[c][...]
            m_new = jnp.maximum(m_prev, jnp.max(s, axis=-1, keepdims=True))
            alpha = jnp.exp2(m_prev - m_new)
            p = jnp.exp2(s - _lane_tile(m_new, s.shape[1]))
            ls[c][...] = alpha * ls[c][...] + jnp.sum(p, axis=-1, keepdims=True)
            pv = jnp.dot(p.astype(BF16), v_blk, preferred_element_type=F32)
            accs[c][...] = _lane_tile(alpha, pv.shape[1]) * accs[c][...] + pv
            ms[c][...] = m_new

    def body(j, carry):
        st = pl.multiple_of(j * tk, tk)
        step(k_ref[pl.ds(st, tk), :], v_ref[pl.ds(st, tk), :])
        return carry

    lax.fori_loop(0, n // tk, body, 0, unroll=unroll)
    step(kc_ref[...], vc_ref[...])

    lam = (jnp.exp(jnp.sum(lq1_ref[...] * lk1_ref[...], axis=-1, keepdims=True))
           - jnp.exp(jnp.sum(lq2_ref[...] * lk2_ref[...], axis=-1, keepdims=True)) + lambda_init)
    w = a1.shape[1]
    o = a1[...] / _lane_tile(l1[...], w) - lam * (a2[...] / _lane_tile(l2[...], w))
    o = o * lax.rsqrt(jnp.mean(o * o, axis=-1, keepdims=True) + SUBLN_EPS) * g_ref[...]
    o_ref[...] = (o * (1.0 - lambda_init)).astype(o_ref.dtype)


def _diff_attn(qkv_l, qkv_c, lq1, lk1, lq2, lk2, subln_g, lambda_init):
    b, n, d3 = qkv_l.shape
    d = d3 // 3
    hw = 2 * HEAD_DIM
    nh = d // hw
    ctx = qkv_c.shape[1]
    tq = min(DA_Q_TILE, n)
    tk = min(DA_K_TILE, n)
    vec = lambda a: a.reshape(1, -1).astype(F32)
    small = pl.BlockSpec((1, HEAD_DIM), lambda bi, h, i: (0, 0))
    return pl.pallas_call(
        functools.partial(_diff_kernel, tk=tk, unroll=min(DA_UNROLL, n // tk), lambda_init=lambda_init),
        out_shape=jax.ShapeDtypeStruct((b, n, d), BF16),
        grid=(b, nh, n // tq),
        in_specs=[
            pl.BlockSpec((None, tq, hw), lambda bi, h, i: (bi, i, h)),
            pl.BlockSpec((None, n, hw), lambda bi, h, i: (bi, 0, nh + h)),
            pl.BlockSpec((None, n, hw), lambda bi, h, i: (bi, 0, 2 * nh + h)),
            pl.BlockSpec((None, ctx, hw), lambda bi, h, i: (bi, 0, nh + h)),
            pl.BlockSpec((None, ctx, hw), lambda bi, h, i: (bi, 0, 2 * nh + h)),
            small, small, small, small,
            pl.BlockSpec((1, hw), lambda bi, h, i: (0, 0)),
        ],
        out_specs=pl.BlockSpec((None, tq, hw), lambda bi, h, i: (bi, i, h)),
        scratch_shapes=[pltpu.VMEM((tq, LANES), F32)] * 4 + [pltpu.VMEM((tq, hw), F32)] * 2,
        compiler_params=_params("arbitrary", "arbitrary", "arbitrary"),
        name="diff_attn",
    )(qkv_l, qkv_l, qkv_l, qkv_c, qkv_c, vec(lq1), vec(lk1), vec(lq2), vec(lk2), vec(subln_g))


def _ffn_kernel(idx_ref, h_ref, gate_ref, mod_ref, wg_ref, wu_ref, wd_ref, o_ref,
                gbuf, xlo, xhi, a_ref, gsem, *, nf, tf, tc, tiles_per_batch, mod_row):
    m = pl.program_id(1)
    s = pl.program_id(2)
    nm = pl.num_programs(1)
    tile = pl.program_id(0) * nm + m
    ntiles = pl.num_programs(0) * nm

    def start_gather(tl):
        base = tl * tc

        def body(i, carry):
            src = h_ref.at[pl.ds(idx_ref[base + i], 1), :]
            pltpu.make_async_copy(src, gbuf.at[pl.ds(i, 1), :], gsem).start()
            return carry

        lax.fori_loop(0, tc, body, 0, unroll=8)

    @pl.when((tile == 0) & (s == 0))
    def _():
        start_gather(0)

    @pl.when(s == 0)
    def _():
        pltpu.make_async_copy(h_ref.at[pl.ds(0, tc), :], gbuf, gsem).wait()
        w = gbuf[...]
        xlo[...] = pltpu.bitcast(w << 16, F32).astype(BF16)
        xhi[...] = pltpu.bitcast(w & jnp.uint32(0xFFFF0000), F32).astype(BF16)

    @pl.when((s == nf) & (tile + 1 < ntiles))
    def _():
        start_gather(tile + 1)

    @pl.when(s < nf)
    def _():
        half = xlo.shape[1]
        wg = wg_ref[...].astype(BF16)
        wu = wu_ref[...].astype(BF16)
        g = (jnp.dot(xlo[...], wg[:half], preferred_element_type=F32)
             + jnp.dot(xhi[...], wg[half:], preferred_element_type=F32))
        u = (jnp.dot(xlo[...], wu[:half], preferred_element_type=F32)
             + jnp.dot(xhi[...], wu[half:], preferred_element_type=F32))
        a = (g * (1.0 / (1.0 + jnp.exp(-g))) * u).astype(BF16)
        for f in range(nf):
            @pl.when(s == f)
            def _():
                a_ref[:, f * tf:(f + 1) * tf] = a

    @pl.when(s >= nf)
    def _():
        y = jnp.dot(a_ref[...], wd_ref[...].astype(BF16), preferred_element_type=F32)
        row = m // tiles_per_batch if mod_row is None else mod_row
        o_ref[...] = y * gate_ref[...] * mod_ref[pl.ds(row, 1), :]


def _ffn(h_packed, idx_flat, gate, mod, w_gate, w_up, w_down, layer, *, cap, mod_row=None):
    n, half = h_packed.shape
    d = 2 * half
    e, r, _ = gate.shape
    ff = w_gate.shape[3]
    tc = min(1024, r if mod_row is not None else cap)
    tf = min(256, ff)
    tn = min(1024, d)
    nf = ff // tf
    nd = d // tn
    assert r % tc == 0 and tc <= n and (mod_row is not None or cap % tc == 0)
    tiles_per_batch = max(cap // tc, 1)
    mod_off = (N_MOD - 1) * nd
    return pl.pallas_call(
        functools.partial(_ffn_kernel, nf=nf, tf=tf, tc=tc, tiles_per_batch=tiles_per_batch,
                          mod_row=mod_row),
        out_shape=jax.ShapeDtypeStruct((e, r, d), F32),
        grid_spec=pltpu.PrefetchScalarGridSpec(
            num_scalar_prefetch=1,
            grid=(e, r // tc, nf + nd),
            in_specs=[
                pl.BlockSpec(memory_space=pl.ANY),
                pl.BlockSpec((None, tc, 1), lambda ei, mi, s, idx: (ei, mi, 0)),
                pl.BlockSpec((COND_ROWS, tn), lambda ei, mi, s, idx: (0, mod_off + jnp.maximum(s - nf, 0))),
                pl.BlockSpec((None, None, d, tf),
                             lambda ei, mi, s, idx: (layer, ei, 0, jnp.minimum(s, nf - 1))),
                pl.BlockSpec((None, None, d, tf),
                             lambda ei, mi, s, idx: (layer, ei, 0, jnp.minimum(s, nf - 1))),
                pl.BlockSpec((None, None, ff, tn),
                             lambda ei, mi, s, idx: (layer, ei, 0, jnp.maximum(s - nf, 0))),
            ],
            out_specs=pl.BlockSpec((None, tc, tn), lambda ei, mi, s, idx: (ei, mi, jnp.maximum(s - nf, 0))),
            scratch_shapes=[pltpu.VMEM((tc, half), jnp.uint32), pltpu.VMEM((tc, half), BF16),
                            pltpu.VMEM((tc, half), BF16), pltpu.VMEM((tc, ff), BF16),
                            pltpu.SemaphoreType.DMA(())],
        ),
        compiler_params=pltpu.CompilerParams(dimension_semantics=("arbitrary",) * 3,
                                             vmem_limit_bytes=V7X_VMEM_LIMIT_BYTES,
                                             disable_bounds_checks=True),
        name="expert_ffn",
    )(idx_flat, h_packed, gate, mod, w_gate, w_up, w_down)


def _combine_kernel(idx_ref, y_ref, x_in_ref, x_ref, buf, gsem, ssem, *, tc):
    del x_in_ref
    t = pl.program_id(0)
    nt = pl.num_programs(0)
    slot = t % 2

    def start_all(chunk, sl, scatter):
        base = chunk * tc

        def body(i, carry):
            hbm = x_ref.at[pl.ds(idx_ref[base + i], 1), :]
            vm = buf.at[sl, pl.ds(i, 1), :]
            if scatter:
                pltpu.make_async_copy(vm, hbm, ssem.at[sl]).start()
            else:
                pltpu.make_async_copy(hbm, vm, gsem.at[sl]).start()
            return carry

        lax.fori_loop(0, tc, body, 0, unroll=8)

    def wait_all(sl, scatter):
        whole = x_ref.at[pl.ds(0, tc), :]
        if scatter:
            pltpu.make_async_copy(buf.at[sl], whole, ssem.at[sl]).wait()
        else:
            pltpu.make_async_copy(whole, buf.at[sl], gsem.at[sl]).wait()

    @pl.when(t == 0)
    def _():
        start_all(0, 0, False)

    @pl.when(t > 0)
    def _():
        wait_all(1 - slot, True)

    @pl.when(t + 1 < nt)
    def _():
        start_all(t + 1, 1 - slot, False)

    wait_all(slot, False)
    buf[slot] = buf[slot] + y_ref[...]
    start_all(t, slot, True)

    @pl.when(t == nt - 1)
    def _():
        wait_all(slot, True)


def _combine(x, y, idx_flat, cap):
    d = x.shape[1]
    e, r, _ = y.shape
    tc = min(512, cap)
    assert cap % tc == 0 and (r // cap >= 2 or cap // tc >= 2)
    nchunk = r // tc
    return pl.pallas_call(
        functools.partial(_combine_kernel, tc=tc),
        out_shape=jax.ShapeDtypeStruct(x.shape, F32),
        grid_spec=pltpu.PrefetchScalarGridSpec(
            num_scalar_prefetch=1,
            grid=(e * nchunk,),
            in_specs=[
                pl.BlockSpec((None, tc, d), lambda t, idx: (t // nchunk, t % nchunk, 0)),
                pl.BlockSpec(memory_space=pl.ANY),
            ],
            out_specs=pl.BlockSpec(memory_space=pl.ANY),
            scratch_shapes=[pltpu.VMEM((2, tc, d), F32), pltpu.SemaphoreType.DMA((2,)),
                            pltpu.SemaphoreType.DMA((2,))],
        ),
        input_output_aliases={2: 0},
        compiler_params=pltpu.CompilerParams(dimension_semantics=("arbitrary",),
                                             vmem_limit_bytes=V7X_VMEM_LIMIT_BYTES,
                                             disable_bounds_checks=True),
        name="moe_combine",
    )(idx_flat, y, x)


def _ec_moe(x, h, aff, mod, w_gate, w_up, w_down, layer, mod_row=None):
    b, n, d = x.shape
    e = aff.shape[1]
    cap = EC_CAPACITY_FACTOR * n // e
    gate, idx = lax.top_k(aff, cap)
    rows = idx + (jnp.arange(b, dtype=jnp.int32) * n)[:, None, None]
    idx_flat = rows.transpose(1, 0, 2).reshape(-1)
    gate_t = gate.transpose(1, 0, 2).reshape(e, b * cap, 1)
    y = _ffn(h.reshape(b * n, d // 2), idx_flat, gate_t, mod, w_gate, w_up, w_down, layer, cap=cap,
             mod_row=mod_row)
    return _combine(x.reshape(b * n, d), y, idx_flat, cap).reshape(b, n, d)


def _rope_tables(n):
    t = np.arange(n, dtype=np.int32)
    row = (t // GRID_W).astype(np.float32)
    col = (t % GRID_W).astype(np.float32)
    pairs = HEAD_DIM // 4
    freq = jnp.asarray(ROPE_BASE, F32) ** (-jnp.arange(pairs, dtype=F32) / pairs)
    ang = jnp.concatenate([jnp.asarray(row)[:, None] * freq, jnp.asarray(col)[:, None] * freq], axis=-1)
    cos, sin = jnp.cos(ang), jnp.sin(ang)
    cos = jnp.stack([cos, cos], axis=-1).reshape(n, HEAD_DIM)
    sin = jnp.stack([-sin, sin], axis=-1).reshape(n, HEAD_DIM)
    return (jnp.stack([cos * Q_SCALE, cos, jnp.ones_like(cos)]),
            jnp.stack([sin * Q_SCALE, sin, jnp.zeros_like(sin)]))


def _scale_table():
    return jnp.broadcast_to(jnp.asarray([Q_SCALE, 1.0, 1.0], F32)[:, None, None], (3, 1, LANES))


def kernel(x, c, ctx, c_ctx, ada_w, ada_b, norm1_g, norm2_g, na_w_qkv, na_w_o, na_rpb, da_w_qkv, da_w_o, da_lambda_q1, da_lambda_k1, da_lambda_q2, da_lambda_k2, da_subln_g, moe_w_router, moe_w_gate, moe_w_up, moe_w_down, final_g):
    b, n, d = x.shape
    n_ctx = ctx.shape[1]
    depth = ada_w.shape[0]
    rows = n // GRID_W
    assert n % GRID_W == 0 and rows % NA_ROW_BLOCK == 0 and rows // NA_ROW_BLOCK >= 3
    assert b + 1 <= COND_ROWS

    cond = jnp.zeros((COND_ROWS, d), F32).at[:b].set(c).at[b].set(c_ctx)
    ada_b3 = ada_b.reshape(depth, 1, N_MOD * d)
    rope = _rope_tables(n)
    plain = _scale_table()

    for i in range(depth):
        last = i == depth - 1
        j = i // N_MIXERS
        mod = _adaln(cond, ada_w, ada_b3, i)
        h_l = _prep(x, norm1_g[i], mod, shift_chunk=0, scale_chunk=1).reshape(b * n, d)
        h_c = _prep(ctx, norm1_g[i], mod, shift_chunk=0, scale_chunk=1, mod_row=b).reshape(b * n_ctx, d)
        if i % N_MIXERS == 0:
            w_o = na_w_o
            qkv_l = _matmul(h_l, na_w_qkv, j, mode="qkv", tables=plain).reshape(b, n, 3 * d)
            qkv_c = _matmul(h_c, na_w_qkv, j, mode="qkv", tables=plain).reshape(b, n_ctx, 3 * d)
            o_l = _natten(qkv_l, qkv_c, _natten_bias(na_rpb[j], rows))
            o_c = None if last else _ctx_attn(qkv_c)
        else:
            w_o = da_w_o
            qkv_l = _matmul(h_l, da_w_qkv, j, mode="qkv", tables=rope, rows_per_batch=n).reshape(b, n, 3 * d)
            qkv_c = _matmul(h_c, da_w_qkv, j, mode="qkv", tables=plain).reshape(b, n_ctx, 3 * d)
            o_l = _diff_attn(qkv_l, qkv_c, da_lambda_q1[j], da_lambda_k1[j], da_lambda_q2[j],
                             da_lambda_k2[j], da_subln_g[j], _lambda_init(i))
            assert last, "context update after a differential layer is not needed at this depth"
            o_c = None
        x = _matmul(o_l.reshape(b * n, d), w_o, j, mode="resid", out_dtype=F32, resid=x.reshape(b * n, d),
                    mod=mod, gate_chunk=2, rows_per_batch=n).reshape(b, n, d)
        router_t = moe_w_router[i].T
        h2, aff = _prep(x, norm2_g[i], mod, shift_chunk=3, scale_chunk=4, router_t=router_t,
                        out_dtype=jnp.uint32)
        x = _ec_moe(x, h2, aff, mod, moe_w_gate, moe_w_up, moe_w_down, i)
        if not last:
            ctx = _matmul(o_c.reshape(b * n_ctx, d), w_o, j, mode="resid", out_dtype=F32,
                          resid=ctx.reshape(b * n_ctx, d), mod=mod, gate_chunk=2,
                          rows_per_batch=n_ctx, mod_row=b).reshape(b, n_ctx, d)
            h2c, affc = _prep(ctx, norm2_g[i], mod, shift_chunk=3, scale_chunk=4, mod_row=b,
                              router_t=router_t, out_dtype=jnp.uint32)
            ctx = _ec_moe(ctx, h2c, affc, mod, moe_w_gate, moe_w_up, moe_w_down, i, mod_row=b)
    return _prep(x, final_g, None, out_dtype=F32)
```

```python
import functools
import math

import numpy as np
import jax
import jax.numpy as jnp
from jax import lax
from jax.experimental import pallas as pl
from jax.experimental.pallas import tpu as pltpu

F32 = jnp.float32
BF16 = jnp.bfloat16

GRID_W = 64
WIN_ROWS = 8
WIN_COLS = 16
HEAD_DIM = 128
N_MIXERS = 2
N_MOD = 6
ROPE_BASE = 10000.0
EC_CAPACITY_FACTOR = 2
NORM_EPS = 1e-6
SUBLN_EPS = 1e-5
NEG_INF = -1e30
LOG2E = math.log2(math.e)
Q_SCALE = HEAD_DIM ** -0.5 * LOG2E

V7X_VMEM_LIMIT_BYTES = 56 * 1024 * 1024
LANES = 128
COND_ROWS = 8
NA_ROW_BLOCK = 4
NA_KEY_ROWS = NA_ROW_BLOCK + WIN_ROWS - 1
NA_HEADS_PER_STEP = 4
DA_Q_TILE = 1024
DA_K_TILE = 512
DA_UNROLL = 8
MM_ROW_CHUNK = 256


def _params(*sem):
    return pltpu.CompilerParams(dimension_semantics=sem, vmem_limit_bytes=V7X_VMEM_LIMIT_BYTES)


def _lambda_init(layer_idx):
    return 0.8 - 0.6 * math.exp(-0.3 * layer_idx)


def _adaln_kernel(s_ref, w_ref, b_ref, o_ref):
    s = s_ref[...]
    s = s * (1.0 / (1.0 + jnp.exp(-s)))
    acc = jnp.dot(s.astype(BF16), w_ref[...].astype(BF16), preferred_element_type=F32)
    o_ref[...] = acc + b_ref[...]


def _adaln(cond, ada_w, ada_b3, layer):
    d = cond.shape[1]
    n6 = ada_w.shape[2]
    tn = min(512, n6)
    return pl.pallas_call(
        _adaln_kernel,
        out_shape=jax.ShapeDtypeStruct((COND_ROWS, n6), F32),
        grid=(n6 // tn,),
        in_specs=[
            pl.BlockSpec((COND_ROWS, d), lambda j: (0, 0)),
            pl.BlockSpec((None, d, tn), lambda j: (layer, 0, j)),
            pl.BlockSpec((None, 1, tn), lambda j: (layer, 0, j)),
        ],
        out_specs=pl.BlockSpec((COND_ROWS, tn), lambda j: (0, j)),
        compiler_params=_params("arbitrary"),
        name="adaln",
    )(cond, ada_w, ada_b3)


def _prep_kernel(x_ref, g_ref, *rest, eps, mod_row, modulate, with_router):
    rest = list(rest)
    if modulate:
        sh_ref, sc_ref = rest[:2]
        rest = rest[2:]
    if with_router:
        wr_ref, o_ref, aff_ref = rest
    else:
        (o_ref,) = rest
    x = x_ref[...]
    h = x * lax.rsqrt(jnp.mean(x * x, axis=-1, keepdims=True) + eps) * g_ref[...]
    if modulate:
        row = pl.program_id(0) if mod_row is None else mod_row
        h = h * (1.0 + sc_ref[pl.ds(row, 1), :]) + sh_ref[pl.ds(row, 1), :]
    if o_ref.dtype == jnp.uint32:
        bits = pltpu.bitcast(h.astype(BF16).astype(F32), jnp.uint32)
        half = o_ref.shape[1]
        o_ref[...] = (bits[:, :half] >> 16) | bits[:, half:]
    else:
        o_ref[...] = h.astype(o_ref.dtype)
    if with_router:
        logits = lax.dot_general(wr_ref[...], h, (((1,), (1,)), ((), ())),
                                 precision=lax.Precision.HIGHEST, preferred_element_type=F32)
        e = jnp.exp(logits - jnp.max(logits, axis=0, keepdims=True))
        aff_ref[...] = e / jnp.sum(e, axis=0, keepdims=True)


def _prep(x, g, mod, *, shift_chunk=None, scale_chunk=None, mod_row=None, out_dtype=BF16,
          router_t=None, eps=NORM_EPS):
    b, n, d = x.shape
    tm = min(256, n)
    modulate = shift_chunk is not None
    with_router = router_t is not None
    in_specs = [pl.BlockSpec((None, tm, d), lambda bi, i: (bi, i, 0)),
                pl.BlockSpec((1, d), lambda bi, i: (0, 0))]
    args = [x, g.reshape(1, d)]
    if modulate:
        in_specs += [pl.BlockSpec((COND_ROWS, d), lambda bi, i: (0, shift_chunk)),
                     pl.BlockSpec((COND_ROWS, d), lambda bi, i: (0, scale_chunk))]
        args += [mod, mod]
    d_out = d // 2 if out_dtype == jnp.uint32 else d
    out_shape = [jax.ShapeDtypeStruct((b, n, d_out), out_dtype)]
    out_specs = [pl.BlockSpec((None, tm, d_out), lambda bi, i: (bi, i, 0))]
    if with_router:
        e = router_t.shape[0]
        in_specs.append(pl.BlockSpec((e, d), lambda bi, i: (0, 0)))
        args.append(router_t)
        out_shape.append(jax.ShapeDtypeStruct((b, e, n), F32))
        out_specs.append(pl.BlockSpec((None, e, tm), lambda bi, i: (bi, 0, i)))
    out = pl.pallas_call(
        functools.partial(_prep_kernel, eps=eps, mod_row=mod_row, modulate=modulate,
                          with_router=with_router),
        out_shape=out_shape,
        grid=(b, n // tm),
        in_specs=in_specs,
        out_specs=out_specs,
        compiler_params=_params("arbitrary", "arbitrary"),
        name="prep",
    )(*args)
    return out if with_router else out[0]


def _mm_kernel(a_ref, w_ref, *rest, mode, rope, tiles_per_batch, mod_row):
    o_ref, wb_ref = rest[-2:]
    i = pl.program_id(1)

    @pl.when(i == 0)
    def _():
        wb_ref[...] = w_ref[...].astype(BF16)

    if mode == "resid":
        acc = jnp.dot(a_ref[...], wb_ref[...], preferred_element_type=F32)
        x_ref, gate_ref = rest[:2]
        row = i // tiles_per_batch if mod_row is None else mod_row
        o_ref[...] = x_ref[...] + gate_ref[pl.ds(row, 1), :] * acc
        return

    chunk = min(MM_ROW_CHUNK, o_ref.shape[0])
    if rope:
        even = lax.broadcasted_iota(jnp.int32, (chunk, LANES), 1) % 2 == 0
    else:
        scale = rest[0][...]
    for r in range(o_ref.shape[0] // chunk):
        rows = slice(r * chunk, (r + 1) * chunk)
        acc = jnp.dot(a_ref[rows, :], wb_ref[...], preferred_element_type=F32)
        if rope:
            cos = rest[0][rows, :]
            sin = rest[1][rows, :]
        for c in range(o_ref.shape[1] // LANES):
            xc = acc[:, c * LANES:(c + 1) * LANES]
            if rope:
                partner = jnp.where(even, pltpu.roll(xc, LANES - 1, 1), pltpu.roll(xc, 1, 1))
                xc = xc * cos + partner * sin
            else:
                xc = xc * scale
            o_ref[rows, c * LANES:(c + 1) * LANES] = xc.astype(o_ref.dtype)


def _matmul(a, w, layer, *, mode, out_dtype=BF16, tables=None, resid=None, mod=None, gate_chunk=None,
            rows_per_batch=None, mod_row=None):
    m, k = a.shape
    n = w.shape[2]
    tm = min(1024, m)
    tn = min(512, n)
    if rows_per_batch is not None:
        tm = min(tm, rows_per_batch)
    in_specs = [pl.BlockSpec((tm, k), lambda j, i: (i, 0)),
                pl.BlockSpec((None, k, tn), lambda j, i: (layer, 0, j))]
    args = [a, w]
    tiles_per_batch = 1
    rope = isinstance(tables, tuple)
    if mode == "qkv":
        nq_tiles = n // 3 // tn
        if rope:
            nblk = tables[0].shape[1] // tm
            spec = pl.BlockSpec((None, tm, LANES), lambda j, i: (j // nq_tiles, i % nblk, 0))
            in_specs += [spec, spec]
            args += list(tables)
        else:
            in_specs.append(pl.BlockSpec((None, 1, LANES), lambda j, i: (j // nq_tiles, 0, 0)))
            args.append(tables)
    else:
        tiles_per_batch = rows_per_batch // tm
        gate_off = gate_chunk * (n // tn)
        in_specs += [pl.BlockSpec((tm, tn), lambda j, i: (i, j)),
                     pl.BlockSpec((COND_ROWS, tn), lambda j, i: (0, gate_off + j))]
        args += [resid, mod]
    return pl.pallas_call(
        functools.partial(_mm_kernel, mode=mode, rope=rope, tiles_per_batch=tiles_per_batch,
                          mod_row=mod_row),
        out_shape=jax.ShapeDtypeStruct((m, n), out_dtype),
        grid=(n // tn, m // tm),
        in_specs=in_specs,
        out_specs=pl.BlockSpec((tm, tn), lambda j, i: (i, j)),
        scratch_shapes=[pltpu.VMEM((k, tn), BF16)],
        compiler_params=_params("arbitrary", "arbitrary"),
        name="matmul_" + mode,
    )(*args)


def _natten_kernel(blk_ref, q_ref, k_ref, v_ref, kc_ref, vc_ref, tab_ref, o_ref, bias_ref, *, rows):
    r = pl.program_id(2)
    nblk = pl.num_programs(2)
    ws = jnp.clip(r * NA_ROW_BLOCK - WIN_ROWS // 2, 0, rows - NA_KEY_ROWS)
    start = pl.multiple_of(ws * GRID_W, GRID_W)
    nk = NA_KEY_ROWS * GRID_W
    nt = (((1,), (1,)), ((), ()))

    @pl.when((r <= 1) | (r == nblk - 1))
    def _():
        cls = jnp.where(r == 0, 0, jnp.where(r == nblk - 1, 2, 1))
        for h in range(NA_HEADS_PER_STEP):
            for a in range(NA_ROW_BLOCK):
                for kr in range(NA_KEY_ROWS):
                    blk = blk_ref[(cls * NA_ROW_BLOCK + a) * NA_KEY_ROWS + kr]
                    bias_ref[h, a * GRID_W:(a + 1) * GRID_W, kr * GRID_W:(kr + 1) * GRID_W] = tab_ref[h, blk]

    for h in range(NA_HEADS_PER_STEP):
        sl = slice(h * HEAD_DIM, (h + 1) * HEAD_DIM)
        q = q_ref[:, sl]
        k = k_ref[pl.ds(start, nk), sl]
        v = v_ref[pl.ds(start, nk), sl]
        s = lax.dot_general(q, k, nt, preferred_element_type=F32) + bias_ref[h]
        sc = lax.dot_general(q, kc_ref[:, sl], nt, preferred_element_type=F32)
        m = jnp.maximum(jnp.max(s, axis=-1, keepdims=True), jnp.max(sc, axis=-1, keepdims=True))
        p = jnp.exp2(s - m)
        pc = jnp.exp2(sc - m)
        vc = vc_ref[:, sl]
        ones = jnp.ones((nk, HEAD_DIM), BF16)
        oa = (jnp.dot(p.astype(BF16), jnp.concatenate([v, ones], axis=1), preferred_element_type=F32)
              + jnp.dot(pc.astype(BF16), jnp.concatenate([vc, ones[:vc.shape[0]]], axis=1),
                        preferred_element_type=F32))
        o_ref[:, sl] = (oa[:, :HEAD_DIM] / oa[:, HEAD_DIM:]).astype(o_ref.dtype)


def _natten_bias(rpb, rows):
    nblk = rows // NA_ROW_BLOCK
    kh = WIN_ROWS
    qc = np.arange(GRID_W, dtype=np.int32)
    col_start = np.clip(qc - WIN_COLS // 2, 0, GRID_W - WIN_COLS)
    col_mask = (qc[None, :] >= col_start[:, None]) & (qc[None, :] < col_start[:, None] + WIN_COLS)
    col_idx = np.clip(qc[None, :] - qc[:, None] + WIN_COLS - 1, 0, 2 * WIN_COLS - 2).astype(np.int32)
    row_idx = np.zeros((3, NA_ROW_BLOCK, NA_KEY_ROWS), np.int32)
    valid = np.zeros((3, NA_ROW_BLOCK, NA_KEY_ROWS), bool)
    for cls, blk in enumerate((0, 1, nblk - 1)):
        r0 = blk * NA_ROW_BLOCK
        ws = int(np.clip(r0 - WIN_ROWS // 2, 0, rows - NA_KEY_ROWS))
        for a in range(NA_ROW_BLOCK):
            r = r0 + a
            rs = int(np.clip(r - kh // 2, 0, rows - kh))
            for kr in range(NA_KEY_ROWS):
                key_row = ws + kr
                if rs <= key_row < rs + kh:
                    valid[cls, a, kr] = True
                    row_idx[cls, a, kr] = key_row - r + (WIN_ROWS - 1)
    for blk in range(1, nblk - 1):
        r0 = blk * NA_ROW_BLOCK
        ws = int(np.clip(r0 - WIN_ROWS // 2, 0, rows - NA_KEY_ROWS))
        for a in range(NA_ROW_BLOCK):
            rs = int(np.clip(r0 + a - kh // 2, 0, rows - kh))
            assert rs - ws == a and r0 - ws == WIN_ROWS // 2
    outside = 2 * WIN_ROWS - 1
    blk_ids = np.where(valid, row_idx, outside).astype(np.int32).reshape(-1)
    t = jnp.where(col_mask[None, None], rpb[:, :, col_idx].astype(F32) * LOG2E, NEG_INF)
    t = jnp.concatenate([t, jnp.full((rpb.shape[0], 1, GRID_W, GRID_W), NEG_INF, F32)], axis=1)
    return jnp.asarray(blk_ids), t


def _natten(qkv_l, qkv_c, bias):
    blk_ids, tab = bias
    b, n, d3 = qkv_l.shape
    d = d3 // 3
    hw = NA_HEADS_PER_STEP * HEAD_DIM
    ng = d // hw
    ctx = qkv_c.shape[1]
    rows = n // GRID_W
    nblk = rows // NA_ROW_BLOCK
    tq = NA_ROW_BLOCK * GRID_W
    nk = NA_KEY_ROWS * GRID_W

    hps = NA_HEADS_PER_STEP
    return pl.pallas_call(
        functools.partial(_natten_kernel, rows=rows),
        out_shape=jax.ShapeDtypeStruct((b, n, d), BF16),
        grid_spec=pltpu.PrefetchScalarGridSpec(
            num_scalar_prefetch=1,
            grid=(b, ng, nblk),
            in_specs=[
                pl.BlockSpec((None, tq, hw), lambda bi, h, r, ids: (bi, r, h)),
                pl.BlockSpec((None, n, hw), lambda bi, h, r, ids: (bi, 0, ng + h)),
                pl.BlockSpec((None, n, hw), lambda bi, h, r, ids: (bi, 0, 2 * ng + h)),
                pl.BlockSpec((None, ctx, hw), lambda bi, h, r, ids: (bi, 0, ng + h)),
                pl.BlockSpec((None, ctx, hw), lambda bi, h, r, ids: (bi, 0, 2 * ng + h)),
                pl.BlockSpec((hps, 2 * WIN_ROWS, GRID_W, GRID_W), lambda bi, h, r, ids: (h, 0, 0, 0)),
            ],
            out_specs=pl.BlockSpec((None, tq, hw), lambda bi, h, r, ids: (bi, r, h)),
            scratch_shapes=[pltpu.VMEM((hps, tq, nk), F32)],
        ),
        compiler_params=_params("arbitrary", "arbitrary", "arbitrary"),
        name="natten",
    )(blk_ids, qkv_l, qkv_l, qkv_l, qkv_c, qkv_c, tab)


def _ctx_attn_kernel(q_ref, k_ref, v_ref, o_ref):
    s = lax.dot_general(q_ref[...], k_ref[...], (((1,), (1,)), ((), ())), preferred_element_type=F32)
    p = jnp.exp2(s - jnp.max(s, axis=-1, keepdims=True))
    l = jnp.sum(p, axis=-1, keepdims=True)
    o = jnp.dot(p.astype(BF16), v_ref[...], preferred_element_type=F32)
    o_ref[...] = (o / l).astype(o_ref.dtype)


def _ctx_attn(qkv_c):
    b, ctx, d3 = qkv_c.shape
    d = d3 // 3
    nh = d // HEAD_DIM
    return pl.pallas_call(
        _ctx_attn_kernel,
        out_shape=jax.ShapeDtypeStruct((b, ctx, d), BF16),
        grid=(b, nh),
        in_specs=[
            pl.BlockSpec((None, ctx, HEAD_DIM), lambda bi, h: (bi, 0, h)),
            pl.BlockSpec((None, ctx, HEAD_DIM), lambda bi, h: (bi, 0, nh + h)),
            pl.BlockSpec((None, ctx, HEAD_DIM), lambda bi, h: (bi, 0, 2 * nh + h)),
        ],
        out_specs=pl.BlockSpec((None, ctx, HEAD_DIM), lambda bi, h: (bi, 0, h)),
        compiler_params=_params("arbitrary", "arbitrary"),
        name="ctx_attn",
    )(qkv_c, qkv_c, qkv_c)


def _lane_tile(x, width):
    return jnp.concatenate([x] * (width // LANES), axis=1)


def _diff_kernel(q_ref, k_ref, v_ref, kc_ref, vc_ref, lq1_ref, lk1_ref, lq2_ref, lk2_ref, g_ref,
                 o_ref, m1, m2, l1, l2, a1, a2, *, tk, unroll, lambda_init):
    n = k_ref.shape[0]
    nt = (((1,), (1,)), ((), ()))
    ms, ls, accs = (m1, m2), (l1, l2), (a1, a2)
    for c in range(2):
        ms[c][...] = jnp.full(ms[c].shape, NEG_INF, F32)
        ls[c][...] = jnp.zeros(ls[c].shape, F32)
        accs[c][...] = jnp.zeros(accs[c].shape, F32)

    def step(k_blk, v_blk):
        for c in range(2):
            q = q_ref[:, c * HEAD_DIM:(c + 1) * HEAD_DIM]
            s = lax.dot_general(q, k_blk[:, c * HEAD_DIM:(c + 1) * HEAD_DIM], nt,
                                preferred_element_type=F32)
            m_prev = ms[c][...]
            m_new = jnp.maximum(m_prev, jnp.max(s, axis=-1, keepdims=True))
            alpha = jnp.exp2(m_prev - m_new)
            p = jnp.exp2(s - _lane_tile(m_new, s.shape[1]))
            ls[c][...] = alpha * ls[c][...] + jnp.sum(p, axis=-1, keepdims=True)
            pv = jnp.dot(p.astype(BF16), v_blk, preferred_element_type=F32)
            accs[c][...] = _lane_tile(alpha, pv.shape[1]) * accs[c][...] + pv
            ms[c][...] = m_new

    def body(j, carry):
        st = pl.multiple_of(j * tk, tk)
        step(k_ref[pl.ds(st, tk), :], v_ref[pl.ds(st, tk), :])
        return carry

    lax.fori_loop(0, n // tk, body, 0, unroll=unroll)
    step(kc_ref[...], vc_ref[...])

    lam = (jnp.exp(jnp.sum(lq1_ref[...] * lk1_ref[...], axis=-1, keepdims=True))
           - jnp.exp(jnp.sum(lq2_ref[...] * lk2_ref[...], axis=-1, keepdims=True)) + lambda_init)
    w = a1.shape[1]
    o = a1[...] / _lane_tile(l1[...], w) - lam * (a2[...] / _lane_tile(l2[...], w))
    o = o * lax.rsqrt(jnp.mean(o * o, axis=-1, keepdims=True) + SUBLN_EPS) * g_ref[...]
    o_ref[...] = (o * (1.0 - lambda_init)).astype(o_ref.dtype)


def _diff_attn(qkv_l, qkv_c, lq1, lk1, lq2, lk2, subln_g, lambda_init):
    b, n, d3 = qkv_l.shape
    d = d3 // 3
    hw = 2 * HEAD_DIM
    nh = d // hw
    ctx = qkv_c.shape[1]
    tq = min(DA_Q_TILE, n)
    tk = min(DA_K_TILE, n)
    vec = lambda a: a.reshape(1, -1).astype(F32)
    small = pl.BlockSpec((1, HEAD_DIM), lambda bi, h, i: (0, 0))
    return pl.pallas_call(
        functools.partial(_diff_kernel, tk=tk, unroll=min(DA_UNROLL, n // tk), lambda_init=lambda_init),
        out_shape=jax.ShapeDtypeStruct((b, n, d), BF16),
        grid=(b, nh, n // tq),
        in_specs=[
            pl.BlockSpec((None, tq, hw), lambda bi, h, i: (bi, i, h)),
            pl.BlockSpec((None, n, hw), lambda bi, h, i: (bi, 0, nh + h)),
            pl.BlockSpec((None, n, hw), lambda bi, h, i: (bi, 0, 2 * nh + h)),
            pl.BlockSpec((None, ctx, hw), lambda bi, h, i: (bi, 0, nh + h)),
            pl.BlockSpec((None, ctx, hw), lambda bi, h, i: (bi, 0, 2 * nh + h)),
            small, small, small, small,
            pl.BlockSpec((1, hw), lambda bi, h, i: (0, 0)),
        ],
        out_specs=pl.BlockSpec((None, tq, hw), lambda bi, h, i: (bi, i, h)),
        scratch_shapes=[pltpu.VMEM((tq, LANES), F32)] * 4 + [pltpu.VMEM((tq, hw), F32)] * 2,
        compiler_params=_params("arbitrary", "arbitrary", "arbitrary"),
        name="diff_attn",
    )(qkv_l, qkv_l, qkv_l, qkv_c, qkv_c, vec(lq1), vec(lk1), vec(lq2), vec(lk2), vec(subln_g))


def _ffn_kernel(idx_ref, h_ref, gate_ref, mod_ref, wg_ref, wu_ref, wd_ref, o_ref,
                gbuf, xlo, xhi, a_ref, gsem, *, nf, nd, tf, tc, tiles_per_batch, mod_row):
    m = pl.program_id(1)
    s = pl.program_id(2)
    nm = pl.num_programs(1)
    tile = pl.program_id(0) * nm + m
    ntiles = pl.num_programs(0) * nm

    def start_gather(tl):
        base = tl * tc

        def body(i, carry):
            src = h_ref.at[pl.ds(idx_ref[base + i], 1), :]
            pltpu.make_async_copy(src, gbuf.at[pl.ds(i, 1), :], gsem).start()
            return carry

        lax.fori_loop(0, tc, body, 0, unroll=8)

    @pl.when((tile == 0) & (s == 0))
    def _():
        start_gather(0)

    @pl.when(s == 0)
    def _():
        pltpu.make_async_copy(h_ref.at[pl.ds(0, tc), :], gbuf, gsem).wait()
        w = gbuf[...]
        xlo[...] = pltpu.bitcast(w << 16, F32).astype(BF16)
        xhi[...] = pltpu.bitcast(w & jnp.uint32(0xFFFF0000), F32).astype(BF16)

    @pl.when(s < nf)
    def _():
        half = xlo.shape[1]
        wg = wg_ref[...].astype(BF16)
        wu = wu_ref[...].astype(BF16)
        g = (jnp.dot(xlo[...], wg[:half], preferred_element_type=F32)
             + jnp.dot(xhi[...], wg[half:], preferred_element_type=F32))
        u = (jnp.dot(xlo[...], wu[:half], preferred_element_type=F32)
             + jnp.dot(xhi[...], wu[half:], preferred_element_type=F32))
        a = (g * (1.0 / (1.0 + jnp.exp(-g))) * u).astype(BF16)
        fe = _ffn_tile_order(m, s, nf)
        for f in range(nf):
            @pl.when(fe == f)
            def _():
                a_ref[:, f * tf:(f + 1) * tf] = a

    @pl.when(s >= nf)
    def _():
        per = tc // nd
        off = (s - nf) * per
        base = jnp.minimum(tile + 1, ntiles - 1) * tc + off
        for i in range(per):
            src = h_ref.at[pl.ds(idx_ref[base + i], 1), :]
            pltpu.make_async_copy(src, gbuf.at[pl.ds(off + i, 1), :], gsem).start()
        y = jnp.dot(a_ref[...], wd_ref[...].astype(BF16), preferred_element_type=F32)
        row = m // tiles_per_batch if mod_row is None else mod_row
        o_ref[...] = (y * gate_ref[...] * mod_ref[pl.ds(row, 1), :]).astype(o_ref.dtype)

    @pl.when((tile == ntiles - 1) & (s == nf + nd - 1))
    def _():
        pltpu.make_async_copy(h_ref.at[pl.ds(0, tc), :], gbuf, gsem).wait()


def _ffn_tile_order(m, s, nf):
    return jnp.clip(jnp.where(m % 2 == 1, nf - 1 - s, s), 0, nf - 1)


def _ffn(h_packed, idx_flat, gate, mod, w_gate, w_up, w_down, layer, *, cap, mod_row=None):
    n, half = h_packed.shape
    d = 2 * half
    e, r, _ = gate.shape
    ff = w_gate.shape[3]
    tc = min(1024, r if mod_row is not None else cap)
    tf = min(256, ff)
    tn = min(1024, d)
    nf = ff // tf
    nd = d // tn
    assert r % tc == 0 and tc <= n and tc % nd == 0 and (mod_row is not None or cap % tc == 0)
    tiles_per_batch = max(cap // tc, 1)
    mod_off = (N_MOD - 1) * nd
    return pl.pallas_call(
        functools.partial(_ffn_kernel, nf=nf, nd=nd, tf=tf, tc=tc, tiles_per_batch=tiles_per_batch,
                          mod_row=mod_row),
        out_shape=jax.ShapeDtypeStruct((e, r, d), BF16),
        grid_spec=pltpu.PrefetchScalarGridSpec(
            num_scalar_prefetch=1,
            grid=(e, r // tc, nf + nd),
            in_specs=[
                pl.BlockSpec(memory_space=pl.ANY),
                pl.BlockSpec((None, tc, 1), lambda ei, mi, s, idx: (ei, mi, 0)),
                pl.BlockSpec((COND_ROWS, tn), lambda ei, mi, s, idx: (0, mod_off + jnp.maximum(s - nf, 0))),
                pl.BlockSpec((None, None, d, tf),
                             lambda ei, mi, s, idx: (layer, ei, 0, _ffn_tile_order(mi, s, nf))),
                pl.BlockSpec((None, None, d, tf),
                             lambda ei, mi, s, idx: (layer, ei, 0, _ffn_tile_order(mi, s, nf))),
                pl.BlockSpec((None, None, ff, tn),
                             lambda ei, mi, s, idx: (layer, ei, 0, jnp.maximum(s - nf, 0))),
            ],
            out_specs=pl.BlockSpec((None, tc, tn), lambda ei, mi, s, idx: (ei, mi, jnp.maximum(s - nf, 0))),
            scratch_shapes=[pltpu.VMEM((tc, half), jnp.uint32), pltpu.VMEM((tc, half), BF16),
                            pltpu.VMEM((tc, half), BF16), pltpu.VMEM((tc, ff), BF16),
                            pltpu.SemaphoreType.DMA(())],
        ),
        compiler_params=pltpu.CompilerParams(dimension_semantics=("arbitrary",) * 3,
                                             vmem_limit_bytes=V7X_VMEM_LIMIT_BYTES,
                                             disable_bounds_checks=True),
        name="expert_ffn",
    )(idx_flat, h_packed, gate, mod, w_gate, w_up, w_down)


def _combine_kernel(idx_ref, y_ref, x_in_ref, x_ref, buf, gsem, ssem, *, tc):
    del x_in_ref
    t = pl.program_id(0)
    nt = pl.num_programs(0)
    slot = t % 2

    def start_all(chunk, sl, scatter):
        base = chunk * tc

        def body(i, carry):
            hbm = x_ref.at[pl.ds(idx_ref[base + i], 1), :]
            vm = buf.at[sl, pl.ds(i, 1), :]
            if scatter:
                pltpu.make_async_copy(vm, hbm, ssem.at[sl]).start()
            else:
                pltpu.make_async_copy(hbm, vm, gsem.at[sl]).start()
            return carry

        lax.fori_loop(0, tc, body, 0, unroll=8)

    def wait_all(sl, scatter):
        whole = x_ref.at[pl.ds(0, tc), :]
        if scatter:
            pltpu.make_async_copy(buf.at[sl], whole, ssem.at[sl]).wait()
        else:
            pltpu.make_async_copy(whole, buf.at[sl], gsem.at[sl]).wait()

    @pl.when(t == 0)
    def _():
        start_all(0, 0, False)

    @pl.when(t > 0)
    def _():
        wait_all(1 - slot, True)

    @pl.when(t + 1 < nt)
    def _():
        start_all(t + 1, 1 - slot, False)

    wait_all(slot, False)
    buf[slot] = buf[slot] + y_ref[...].astype(F32)
    start_all(t, slot, True)

    @pl.when(t == nt - 1)
    def _():
        wait_all(slot, True)


def _combine(x, y, idx_flat, cap):
    d = x.shape[1]
    e, r, _ = y.shape
    tc = min(512, cap)
    assert cap % tc == 0 and (r // cap >= 2 or cap // tc >= 2)
    nchunk = r // tc
    return pl.pallas_call(
        functools.partial(_combine_kernel, tc=tc),
        out_shape=jax.ShapeDtypeStruct(x.shape, F32),
        grid_spec=pltpu.PrefetchScalarGridSpec(
            num_scalar_prefetch=1,
            grid=(e * nchunk,),
            in_specs=[
                pl.BlockSpec((None, tc, d), lambda t, idx: (t // nchunk, t % nchunk, 0)),
                pl.BlockSpec(memory_space=pl.ANY),
            ],
            out_specs=pl.BlockSpec(memory_space=pl.ANY),
            scratch_shapes=[pltpu.VMEM((2, tc, d), F32), pltpu.SemaphoreType.DMA((2,)),
                            pltpu.SemaphoreType.DMA((2,))],
        ),
        input_output_aliases={2: 0},
        compiler_params=pltpu.CompilerParams(dimension_semantics=("arbitrary",),
                                             vmem_limit_bytes=V7X_VMEM_LIMIT_BYTES,
                                             disable_bounds_checks=True),
        name="moe_combine",
    )(idx_flat, y, x)


def _ec_moe(x, h, aff, mod, w_gate, w_up, w_down, layer, mod_row=None):
    b, n, d = x.shape
    e = aff.shape[1]
    cap = EC_CAPACITY_FACTOR * n // e
    gate, idx = lax.top_k(aff, cap)
    rows = idx + (jnp.arange(b, dtype=jnp.int32) * n)[:, None, None]
    idx_flat = rows.transpose(1, 0, 2).reshape(-1)
    gate_t = gate.transpose(1, 0, 2).reshape(e, b * cap, 1)
    y = _ffn(h.reshape(b * n, d // 2), idx_flat, gate_t, mod, w_gate, w_up, w_down, layer, cap=cap,
             mod_row=mod_row)
    return _combine(x.reshape(b * n, d), y, idx_flat, cap).reshape(b, n, d)


def _rope_tables(n):
    t = np.arange(n, dtype=np.int32)
    row = (t // GRID_W).astype(np.float32)
    col = (t % GRID_W).astype(np.float32)
    pairs = HEAD_DIM // 4
    freq = jnp.asarray(ROPE_BASE, F32) ** (-jnp.arange(pairs, dtype=F32) / pairs)
    ang = jnp.concatenate([jnp.asarray(row)[:, None] * freq, jnp.asarray(col)[:, None] * freq], axis=-1)
    cos, sin = jnp.cos(ang), jnp.sin(ang)
    cos = jnp.stack([cos, cos], axis=-1).reshape(n, HEAD_DIM)
    sin = jnp.stack([-sin, sin], axis=-1).reshape(n, HEAD_DIM)
    return (jnp.stack([cos * Q_SCALE, cos, jnp.ones_like(cos)]),
            jnp.stack([sin * Q_SCALE, sin, jnp.zeros_like(sin)]))


def _scale_table():
    return jnp.broadcast_to(jnp.asarray([Q_SCALE, 1.0, 1.0], F32)[:, None, None], (3, 1, LANES))


def kernel(x, c, ctx, c_ctx, ada_w, ada_b, norm1_g, norm2_g, na_w_qkv, na_w_o, na_rpb, da_w_qkv, da_w_o, da_lambda_q1, da_lambda_k1, da_lambda_q2, da_lambda_k2, da_subln_g, moe_w_router, moe_w_gate, moe_w_up, moe_w_down, final_g):
    b, n, d = x.shape
    n_ctx = ctx.shape[1]
    depth = ada_w.shape[0]
    rows = n // GRID_W
    assert n % GRID_W == 0 and rows % NA_ROW_BLOCK == 0 and rows // NA_ROW_BLOCK >= 3
    assert b + 1 <= COND_ROWS

    cond = jnp.zeros((COND_ROWS, d), F32).at[:b].set(c).at[b].set(c_ctx)
    ada_b3 = ada_b.reshape(depth, 1, N_MOD * d)
    rope = _rope_tables(n)
    plain = _scale_table()

    for i in range(depth):
        last = i == depth - 1
        j = i // N_MIXERS
        mod = _adaln(cond, ada_w, ada_b3, i)
        h_l = _prep(x, norm1_g[i], mod, shift_chunk=0, scale_chunk=1).reshape(b * n, d)
        h_c = _prep(ctx, norm1_g[i], mod, shift_chunk=0, scale_chunk=1, mod_row=b).reshape(b * n_ctx, d)
        if i % N_MIXERS == 0:
            w_o = na_w_o
            qkv_l = _matmul(h_l, na_w_qkv, j, mode="qkv", tables=plain).reshape(b, n, 3 * d)
            qkv_c = _matmul(h_c, na_w_qkv, j, mode="qkv", tables=plain).reshape(b, n_ctx, 3 * d)
            o_l = _natten(qkv_l, qkv_c, _natten_bias(na_rpb[j], rows))
            o_c = None if last else _ctx_attn(qkv_c)
        else:
            w_o = da_w_o
            qkv_l = _matmul(h_l, da_w_qkv, j, mode="qkv", tables=rope, rows_per_batch=n).reshape(b, n, 3 * d)
            qkv_c = _matmul(h_c, da_w_qkv, j, mode="qkv", tables=plain).reshape(b, n_ctx, 3 * d)
            o_l = _diff_attn(qkv_l, qkv_c, da_lambda_q1[j], da_lambda_k1[j], da_lambda_q2[j],
                             da_lambda_k2[j], da_subln_g[j], _lambda_init(i))
            assert last, "context update after a differential layer is not needed at this depth"
            o_c = None
        x = _matmul(o_l.reshape(b * n, d), w_o, j, mode="resid", out_dtype=F32, resid=x.reshape(b * n, d),
                    mod=mod, gate_chunk=2, rows_per_batch=n).reshape(b, n, d)
        router_t = moe_w_router[i].T
        h2, aff = _prep(x, norm2_g[i], mod, shift_chunk=3, scale_chunk=4, router_t=router_t,
                        out_dtype=jnp.uint32)
        x = _ec_moe(x, h2, aff, mod, moe_w_gate, moe_w_up, moe_w_down, i)
        if not last:
            ctx = _matmul(o_c.reshape(b * n_ctx, d), w_o, j, mode="resid", out_dtype=F32,
                          resid=ctx.reshape(b * n_ctx, d), mod=mod, gate_chunk=2,
                          rows_per_batch=n_ctx, mod_row=b).reshape(b, n_ctx, d)
            h2c, affc = _prep(ctx, norm2_g[i], mod, shift_chunk=3, scale_chunk=4, mod_row=b,
                              router_t=router_t, out_dtype=jnp.uint32)
            ctx = _ec_moe(ctx, h2c, affc, mod, moe_w_gate, moe_w_up, moe_w_down, i, mod_row=b)
    return _prep(x, final_g, None, out_dtype=F32)
```

```python
import functools
import math

import numpy as np
import jax
import jax.numpy as jnp
from jax import lax
from jax.experimental import pallas as pl
from jax.experimental.pallas import tpu as pltpu

F32 = jnp.float32
BF16 = jnp.bfloat16

GRID_W = 64
WIN_ROWS = 8
WIN_COLS = 16
HEAD_DIM = 128
N_MIXERS = 2
N_MOD = 6
ROPE_BASE = 10000.0
EC_CAPACITY_FACTOR = 2
NORM_EPS = 1e-6
SUBLN_EPS = 1e-5
NEG_INF = -1e30
LOG2E = math.log2(math.e)
Q_SCALE = HEAD_DIM ** -0.5 * LOG2E

V7X_VMEM_LIMIT_BYTES = 56 * 1024 * 1024
LANES = 128
COND_ROWS = 8
NA_ROW_BLOCK = 4
NA_KEY_ROWS = NA_ROW_BLOCK + WIN_ROWS - 1
NA_HEADS_PER_STEP = 4
DA_Q_TILE = 1024
DA_K_TILE = 512
DA_UNROLL = 8
MM_ROW_CHUNK = 256


def _params(*sem):
    return pltpu.CompilerParams(dimension_semantics=sem, vmem_limit_bytes=V7X_VMEM_LIMIT_BYTES)


def _lambda_init(layer_idx):
    return 0.8 - 0.6 * math.exp(-0.3 * layer_idx)


def _adaln_kernel(s_ref, w_ref, b_ref, o_ref):
    s = s_ref[...]
    s = s * (1.0 / (1.0 + jnp.exp(-s)))
    acc = jnp.dot(s.astype(BF16), w_ref[...].astype(BF16), preferred_element_type=F32)
    o_ref[...] = acc + b_ref[...]


def _adaln(cond, ada_w, ada_b3, layer):
    d = cond.shape[1]
    n6 = ada_w.shape[2]
    tn = min(512, n6)
    return pl.pallas_call(
        _adaln_kernel,
        out_shape=jax.ShapeDtypeStruct((COND_ROWS, n6), F32),
        grid=(n6 // tn,),
        in_specs=[
            pl.BlockSpec((COND_ROWS, d), lambda j: (0, 0)),
            pl.BlockSpec((None, d, tn), lambda j: (layer, 0, j)),
            pl.BlockSpec((None, 1, tn), lambda j: (layer, 0, j)),
        ],
        out_specs=pl.BlockSpec((COND_ROWS, tn), lambda j: (0, j)),
        compiler_params=_params("arbitrary"),
        name="adaln",
    )(cond, ada_w, ada_b3)


def _prep_kernel(x_ref, g_ref, *rest, eps, mod_row, modulate, with_router):
    rest = list(rest)
    if modulate:
        sh_ref, sc_ref = rest[:2]
        rest = rest[2:]
    if with_router:
        wr_ref, o_ref, aff_ref = rest
    else:
        (o_ref,) = rest
    x = x_ref[...]
    h = x * lax.rsqrt(jnp.mean(x * x, axis=-1, keepdims=True) + eps) * g_ref[...]
    if modulate:
        row = pl.program_id(0) if mod_row is None else mod_row
        h = h * (1.0 + sc_ref[pl.ds(row, 1), :]) + sh_ref[pl.ds(row, 1), :]
    if o_ref.dtype == jnp.uint32:
        bits = pltpu.bitcast(h.astype(BF16).astype(F32), jnp.uint32)
        half = o_ref.shape[1]
        o_ref[...] = (bits[:, :half] >> 16) | bits[:, half:]
    else:
        o_ref[...] = h.astype(o_ref.dtype)
    if with_router:
        logits = lax.dot_general(wr_ref[...], h, (((1,), (1,)), ((), ())),
                                 precision=lax.Precision.HIGHEST, preferred_element_type=F32)
        e = jnp.exp(logits - jnp.max(logits, axis=0, keepdims=True))
        aff_ref[...] = e / jnp.sum(e, axis=0, keepdims=True)


def _prep(x, g, mod, *, shift_chunk=None, scale_chunk=None, mod_row=None, out_dtype=BF16,
          router_t=None, eps=NORM_EPS):
    b, n, d = x.shape
    tm = min(256, n)
    modulate = shift_chunk is not None
    with_router = router_t is not None
    in_specs = [pl.BlockSpec((None, tm, d), lambda bi, i: (bi, i, 0)),
                pl.BlockSpec((1, d), lambda bi, i: (0, 0))]
    args = [x, g.reshape(1, d)]
    if modulate:
        in_specs += [pl.BlockSpec((COND_ROWS, d), lambda bi, i: (0, shift_chunk)),
                     pl.BlockSpec((COND_ROWS, d), lambda bi, i: (0, scale_chunk))]
        args += [mod, mod]
    d_out = d // 2 if out_dtype == jnp.uint32 else d
    out_shape = [jax.ShapeDtypeStruct((b, n, d_out), out_dtype)]
    out_specs = [pl.BlockSpec((None, tm, d_out), lambda bi, i: (bi, i, 0))]
    if with_router:
        e = router_t.shape[0]
        in_specs.append(pl.BlockSpec((e, d), lambda bi, i: (0, 0)))
        args.append(router_t)
        out_shape.append(jax.ShapeDtypeStruct((b, e, n), F32))
        out_specs.append(pl.BlockSpec((None, e, tm), lambda bi, i: (bi, 0, i)))
    out = pl.pallas_call(
        functools.partial(_prep_kernel, eps=eps, mod_row=mod_row, modulate=modulate,
                          with_router=with_router),
        out_shape=out_shape,
        grid=(b, n // tm),
        in_specs=in_specs,
        out_specs=out_specs,
        compiler_params=_params("arbitrary", "arbitrary"),
        name="prep",
    )(*args)
    return out if with_router else out[0]


def _mm_kernel(a_ref, w_ref, *rest, mode, rope, tiles_per_batch, mod_row):
    o_ref, wb_ref = rest[-2:]
    i = pl.program_id(1)

    @pl.when(i == 0)
    def _():
        wb_ref[...] = w_ref[...].astype(BF16)

    if mode == "resid":
        acc = jnp.dot(a_ref[...], wb_ref[...], preferred_element_type=F32)
        x_ref, gate_ref = rest[:2]
        row = i // tiles_per_batch if mod_row is None else mod_row
        o_ref[...] = x_ref[...] + gate_ref[pl.ds(row, 1), :] * acc
        return

    chunk = min(MM_ROW_CHUNK, o_ref.shape[0])
    if rope:
        even = lax.broadcasted_iota(jnp.int32, (chunk, LANES), 1) % 2 == 0
    else:
        scale = rest[0][...]
    for r in range(o_ref.shape[0] // chunk):
        rows = slice(r * chunk, (r + 1) * chunk)
        acc = jnp.dot(a_ref[rows, :], wb_ref[...], preferred_element_type=F32)
        if rope:
            cos = rest[0][rows, :]
            sin = rest[1][rows, :]
        for c in range(o_ref.shape[1] // LANES):
            xc = acc[:, c * LANES:(c + 1) * LANES]
            if rope:
                partner = jnp.where(even, pltpu.roll(xc, LANES - 1, 1), pltpu.roll(xc, 1, 1))
                xc = xc * cos + partner * sin
            else:
                xc = xc * scale
            o_ref[rows, c * LANES:(c + 1) * LANES] = xc.astype(o_ref.dtype)


def _matmul(a, w, layer, *, mode, out_dtype=BF16, tables=None, resid=None, mod=None, gate_chunk=None,
            rows_per_batch=None, mod_row=None):
    m, k = a.shape
    n = w.shape[2]
    tm = min(1024, m)
    tn = min(512, n)
    if rows_per_batch is not None:
        tm = min(tm, rows_per_batch)
    in_specs = [pl.BlockSpec((tm, k), lambda j, i: (i, 0)),
                pl.BlockSpec((None, k, tn), lambda j, i: (layer, 0, j))]
    args = [a, w]
    tiles_per_batch = 1
    rope = isinstance(tables, tuple)
    if mode == "qkv":
        nq_tiles = n // 3 // tn
        if rope:
            nblk = tables[0].shape[1] // tm
            spec = pl.BlockSpec((None, tm, LANES), lambda j, i: (j // nq_tiles, i % nblk, 0))
            in_specs += [spec, spec]
            args += list(tables)
        else:
            in_specs.append(pl.BlockSpec((None, 1, LANES), lambda j, i: (j // nq_tiles, 0, 0)))
            args.append(tables)
    else:
        tiles_per_batch = rows_per_batch // tm
        gate_off = gate_chunk * (n // tn)
        in_specs += [pl.BlockSpec((tm, tn), lambda j, i: (i, j)),
                     pl.BlockSpec((COND_ROWS, tn), lambda j, i: (0, gate_off + j))]
        args += [resid, mod]
    return pl.pallas_call(
        functools.partial(_mm_kernel, mode=mode, rope=rope, tiles_per_batch=tiles_per_batch,
                          mod_row=mod_row),
        out_shape=jax.ShapeDtypeStruct((m, n), out_dtype),
        grid=(n // tn, m // tm),
        in_specs=in_specs,
        out_specs=pl.BlockSpec((tm, tn), lambda j, i: (i, j)),
        scratch_shapes=[pltpu.VMEM((k, tn), BF16)],
        compiler_params=_params("arbitrary", "arbitrary"),
        name="matmul_" + mode,
    )(*args)


def _natten_kernel(blk_ref, q_ref, k_ref, v_ref, kc_ref, vc_ref, tab_ref, o_ref, bias_ref, *, rows):
    r = pl.program_id(2)
    nblk = pl.num_programs(2)
    ws = jnp.clip(r * NA_ROW_BLOCK - WIN_ROWS // 2, 0, rows - NA_KEY_ROWS)
    start = pl.multiple_of(ws * GRID_W, GRID_W)
    nk = NA_KEY_ROWS * GRID_W
    nt = (((1,), (1,)), ((), ()))

    @pl.when((r <= 1) | (r == nblk - 1))
    def _():
        cls = jnp.where(r == 0, 0, jnp.where(r == nblk - 1, 2, 1))
        for h in range(NA_HEADS_PER_STEP):
            for a in range(NA_ROW_BLOCK):
                for kr in range(NA_KEY_ROWS):
                    blk = blk_ref[(cls * NA_ROW_BLOCK + a) * NA_KEY_ROWS + kr]
                    bias_ref[h, a * GRID_W:(a + 1) * GRID_W, kr * GRID_W:(kr + 1) * GRID_W] = tab_ref[h, blk]

    for h in range(NA_HEADS_PER_STEP):
        sl = slice(h * HEAD_DIM, (h + 1) * HEAD_DIM)
        q = q_ref[:, sl]
        k = k_ref[pl.ds(start, nk), sl]
        v = v_ref[pl.ds(start, nk), sl]
        s = lax.dot_general(q, k, nt, preferred_element_type=F32) + bias_ref[h]
        sc = lax.dot_general(q, kc_ref[:, sl], nt, preferred_element_type=F32)
        m = jnp.maximum(jnp.max(s, axis=-1, keepdims=True), jnp.max(sc, axis=-1, keepdims=True))
        p = jnp.exp2(s - m)
        pc = jnp.exp2(sc - m)
        vc = vc_ref[:, sl]
        ones = jnp.ones((nk, HEAD_DIM), BF16)
        oa = (jnp.dot(p.astype(BF16), jnp.concatenate([v, ones], axis=1), preferred_element_type=F32)
              + jnp.dot(pc.astype(BF16), jnp.concatenate([vc, ones[:vc.shape[0]]], axis=1),
                        preferred_element_type=F32))
        o_ref[:, sl] = (oa[:, :HEAD_DIM] / oa[:, HEAD_DIM:]).astype(o_ref.dtype)


def _natten_bias(rpb, rows):
    nblk = rows // NA_ROW_BLOCK
    kh = WIN_ROWS
    qc = np.arange(GRID_W, dtype=np.int32)
    col_start = np.clip(qc - WIN_COLS // 2, 0, GRID_W - WIN_COLS)
    col_mask = (qc[None, :] >= col_start[:, None]) & (qc[None, :] < col_start[:, None] + WIN_COLS)
    col_idx = np.clip(qc[None, :] - qc[:, None] + WIN_COLS - 1, 0, 2 * WIN_COLS - 2).astype(np.int32)
    row_idx = np.zeros((3, NA_ROW_BLOCK, NA_KEY_ROWS), np.int32)
    valid = np.zeros((3, NA_ROW_BLOCK, NA_KEY_ROWS), bool)
    for cls, blk in enumerate((0, 1, nblk - 1)):
        r0 = blk * NA_ROW_BLOCK
        ws = int(np.clip(r0 - WIN_ROWS // 2, 0, rows - NA_KEY_ROWS))
        for a in range(NA_ROW_BLOCK):
            r = r0 + a
            rs = int(np.clip(r - kh // 2, 0, rows - kh))
            for kr in range(NA_KEY_ROWS):
                key_row = ws + kr
                if rs <= key_row < rs + kh:
                    valid[cls, a, kr] = True
                    row_idx[cls, a, kr] = key_row - r + (WIN_ROWS - 1)
    for blk in range(1, nblk - 1):
        r0 = blk * NA_ROW_BLOCK
        ws = int(np.clip(r0 - WIN_ROWS // 2, 0, rows - NA_KEY_ROWS))
        for a in range(NA_ROW_BLOCK):
            rs = int(np.clip(r0 + a - kh // 2, 0, rows - kh))
            assert rs - ws == a and r0 - ws == WIN_ROWS // 2
    outside = 2 * WIN_ROWS - 1
    blk_ids = np.where(valid, row_idx, outside).astype(np.int32).reshape(-1)
    t = jnp.where(col_mask[None, None], rpb[:, :, col_idx].astype(F32) * LOG2E, NEG_INF)
    t = jnp.concatenate([t, jnp.full((rpb.shape[0], 1, GRID_W, GRID_W), NEG_INF, F32)], axis=1)
    return jnp.asarray(blk_ids), t


def _natten(qkv_l, qkv_c, bias):
    blk_ids, tab = bias
    b, n, d3 = qkv_l.shape
    d = d3 // 3
    hw = NA_HEADS_PER_STEP * HEAD_DIM
    ng = d // hw
    ctx = qkv_c.shape[1]
    rows = n // GRID_W
    nblk = rows // NA_ROW_BLOCK
    tq = NA_ROW_BLOCK * GRID_W
    nk = NA_KEY_ROWS * GRID_W

    hps = NA_HEADS_PER_STEP
    return pl.pallas_call(
        functools.partial(_natten_kernel, rows=rows),
        out_shape=jax.ShapeDtypeStruct((b, n, d), BF16),
        grid_spec=pltpu.PrefetchScalarGridSpec(
            num_scalar_prefetch=1,
            grid=(b, ng, nblk),
            in_specs=[
                pl.BlockSpec((None, tq, hw), lambda bi, h, r, ids: (bi, r, h)),
                pl.BlockSpec((None, n, hw), lambda bi, h, r, ids: (bi, 0, ng + h)),
                pl.BlockSpec((None, n, hw), lambda bi, h, r, ids: (bi, 0, 2 * ng + h)),
                pl.BlockSpec((None, ctx, hw), lambda bi, h, r, ids: (bi, 0, ng + h)),
                pl.BlockSpec((None, ctx, hw), lambda bi, h, r, ids: (bi, 0, 2 * ng + h)),
                pl.BlockSpec((hps, 2 * WIN_ROWS, GRID_W, GRID_W), lambda bi, h, r, ids: (h, 0, 0, 0)),
            ],
            out_specs=pl.BlockSpec((None, tq, hw), lambda bi, h, r, ids: (bi, r, h)),
            scratch_shapes=[pltpu.VMEM((hps, tq, nk), F32)],
        ),
        compiler_params=_params("arbitrary", "arbitrary", "arbitrary"),
        name="natten",
    )(blk_ids, qkv_l, qkv_l, qkv_l, qkv_c, qkv_c, tab)


def _ctx_attn_kernel(q_ref, k_ref, v_ref, o_ref):
    s = lax.dot_general(q_ref[...], k_ref[...], (((1,), (1,)), ((), ())), preferred_element_type=F32)
    p = jnp.exp2(s - jnp.max(s, axis=-1, keepdims=True))
    l = jnp.sum(p, axis=-1, keepdims=True)
    o = jnp.dot(p.astype(BF16), v_ref[...], preferred_element_type=F32)
    o_ref[...] = (o / l).astype(o_ref.dtype)


def _ctx_attn(qkv_c):
    b, ctx, d3 = qkv_c.shape
    d = d3 // 3
    nh = d // HEAD_DIM
    return pl.pallas_call(
        _ctx_attn_kernel,
        out_shape=jax.ShapeDtypeStruct((b, ctx, d), BF16),
        grid=(b, nh),
        in_specs=[
            pl.BlockSpec((None, ctx, HEAD_DIM), lambda bi, h: (bi, 0, h)),
            pl.BlockSpec((None, ctx, HEAD_DIM), lambda bi, h: (bi, 0, nh + h)),
            pl.BlockSpec((None, ctx, HEAD_DIM), lambda bi, h: (bi, 0, 2 * nh + h)),
        ],
        out_specs=pl.BlockSpec((None, ctx, HEAD_DIM), lambda bi, h: (bi, 0, h)),
        compiler_params=_params("arbitrary", "arbitrary"),
        name="ctx_attn",
    )(qkv_c, qkv_c, qkv_c)


def _lane_tile(x, width):
    return jnp.concatenate([x] * (width // LANES), axis=1)


def _diff_kernel(q_ref, k_ref, v_ref, kc_ref, vc_ref, lq1_ref, lk1_ref, lq2_ref, lk2_ref, g_ref,
                 o_ref, m1, m2, l1, l2, a1, a2, *, tk, unroll, lambda_init):
    n = k_ref.shape[0]
    nt = (((1,), (1,)), ((), ()))
    ms, ls, accs = (m1, m2), (l1, l2), (a1, a2)
    for c in range(2):
        ms[c][...] = jnp.full(ms[c].shape, NEG_INF, F32)
        ls[c][...] = jnp.zeros(ls[c].shape, F32)
        accs[c][...] = jnp.zeros(accs[c].shape, F32)

    def step(k_blk, v_blk):
        for c in range(2):
            q = q_ref[:, c * HEAD_DIM:(c + 1) * HEAD_DIM]
            s = lax.dot_general(q, k_blk[:, c * HEAD_DIM:(c + 1) * HEAD_DIM], nt,
                                preferred_element_type=F32)
            m_prev = ms[c][...]
            m_new = jnp.maximum(m_prev, jnp.max(s, axis=-1, keepdims=True))
            alpha = jnp.exp2(m_prev - m_new)
            p = jnp.exp2(s - _lane_tile(m_new, s.shape[1]))
            ls[c][...] = alpha * ls[c][...] + jnp.sum(p, axis=-1, keepdims=True)
            pv = jnp.dot(p.astype(BF16), v_blk, preferred_element_type=F32)
            accs[c][...] = _lane_tile(alpha, pv.shape[1]) * accs[c][...] + pv
            ms[c][...] = m_new

    def body(j, carry):
        st = pl.multiple_of(j * tk, tk)
        step(k_ref[pl.ds(st, tk), :], v_ref[pl.ds(st, tk), :])
        return carry

    lax.fori_loop(0, n // tk, body, 0, unroll=unroll)
    step(kc_ref[...], vc_ref[...])

    lam = (jnp.exp(jnp.sum(lq1_ref[...] * lk1_ref[...], axis=-1, keepdims=True))
           - jnp.exp(jnp.sum(lq2_ref[...] * lk2_ref[...], axis=-1, keepdims=True)) + lambda_init)
    w = a1.shape[1]
    o = a1[...] / _lane_tile(l1[...], w) - lam * (a2[...] / _lane_tile(l2[...], w))
    o = o * lax.rsqrt(jnp.mean(o * o, axis=-1, keepdims=True) + SUBLN_EPS) * g_ref[...]
    o_ref[...] = (o * (1.0 - lambda_init)).astype(o_ref.dtype)


def _diff_attn(qkv_l, qkv_c, lq1, lk1, lq2, lk2, subln_g, lambda_init):
    b, n, d3 = qkv_l.shape
    d = d3 // 3
    hw = 2 * HEAD_DIM
    nh = d // hw
    ctx = qkv_c.shape[1]
    tq = min(DA_Q_TILE, n)
    tk = min(DA_K_TILE, n)
    vec = lambda a: a.reshape(1, -1).astype(F32)
    small = pl.BlockSpec((1, HEAD_DIM), lambda bi, h, i: (0, 0))
    return pl.pallas_call(
        functools.partial(_diff_kernel, tk=tk, unroll=min(DA_UNROLL, n // tk), lambda_init=lambda_init),
        out_shape=jax.ShapeDtypeStruct((b, n, d), BF16),
        grid=(b, nh, n // tq),
        in_specs=[
            pl.BlockSpec((None, tq, hw), lambda bi, h, i: (bi, i, h)),
            pl.BlockSpec((None, n, hw), lambda bi, h, i: (bi, 0, nh + h)),
            pl.BlockSpec((None, n, hw), lambda bi, h, i: (bi, 0, 2 * nh + h)),
            pl.BlockSpec((None, ctx, hw), lambda bi, h, i: (bi, 0, nh + h)),
            pl.BlockSpec((None, ctx, hw), lambda bi, h, i: (bi, 0, 2 * nh + h)),
            small, small, small, small,
            pl.BlockSpec((1, hw), lambda bi, h, i: (0, 0)),
        ],
        out_specs=pl.BlockSpec((None, tq, hw), lambda bi, h, i: (bi, i, h)),
        scratch_shapes=[pltpu.VMEM((tq, LANES), F32)] * 4 + [pltpu.VMEM((tq, hw), F32)] * 2,
        compiler_params=_params("arbitrary", "arbitrary", "arbitrary"),
        name="diff_attn",
    )(qkv_l, qkv_l, qkv_l, qkv_c, qkv_c, vec(lq1), vec(lk1), vec(lq2), vec(lk2), vec(subln_g))


def _ffn_kernel(idx_ref, h_ref, gate_ref, mod_ref, wg_ref, wu_ref, wd_ref, o_ref,
                gbuf, xlo, xhi, a_ref, gsem, *, nf, nd, tf, tc, tiles_per_batch, mod_row):
    m = pl.program_id(1)
    s = pl.program_id(2)
    nm = pl.num_programs(1)
    tile = pl.program_id(0) * nm + m
    ntiles = pl.num_programs(0) * nm

    def start_gather(tl):
        base = tl * tc

        def body(i, carry):
            src = h_ref.at[pl.ds(idx_ref[base + i], 1), :]
            pltpu.make_async_copy(src, gbuf.at[pl.ds(i, 1), :], gsem).start()
            return carry

        lax.fori_loop(0, tc, body, 0, unroll=8)

    nsteps = nf + nd
    first_issue_step = 0 if tc % nsteps == 0 else nf
    per = tc // (nsteps - first_issue_step)

    def request_slice():
        off = (s - first_issue_step) * per
        base = jnp.minimum(tile + 1, ntiles - 1) * tc + off
        for i in range(per):
            src = h_ref.at[pl.ds(idx_ref[base + i], 1), :]
            pltpu.make_async_copy(src, gbuf.at[pl.ds(off + i, 1), :], gsem).start()

    @pl.when((tile == 0) & (s == 0))
    def _():
        start_gather(0)

    @pl.when(s == 0)
    def _():
        pltpu.make_async_copy(h_ref.at[pl.ds(0, tc), :], gbuf, gsem).wait()
        w = gbuf[...]
        xlo[...] = pltpu.bitcast(w << 16, F32).astype(BF16)
        xhi[...] = pltpu.bitcast(w & jnp.uint32(0xFFFF0000), F32).astype(BF16)

    @pl.when(s < nf)
    def _():
        half = xlo.shape[1]
        wg = wg_ref[...].astype(BF16)
        wu = wu_ref[...].astype(BF16)
        g = (jnp.dot(xlo[...], wg[:half], preferred_element_type=F32)
             + jnp.dot(xhi[...], wg[half:], preferred_element_type=F32))
        u = (jnp.dot(xlo[...], wu[:half], preferred_element_type=F32)
             + jnp.dot(xhi[...], wu[half:], preferred_element_type=F32))
        if first_issue_step == 0:
            request_slice()
        a = (g * (1.0 / (1.0 + jnp.exp(-g))) * u).astype(BF16)
        fe = _ffn_tile_order(m, s, nf)
        for f in range(nf):
            @pl.when(fe == f)
            def _():
                a_ref[:, f * tf:(f + 1) * tf] = a

    @pl.when(s >= nf)
    def _():
        request_slice()
        y = jnp.dot(a_ref[...], wd_ref[...].astype(BF16), preferred_element_type=F32)
        row = m // tiles_per_batch if mod_row is None else mod_row
        o_ref[...] = (y * gate_ref[...] * mod_ref[pl.ds(row, 1), :]).astype(o_ref.dtype)

    @pl.when((tile == ntiles - 1) & (s == nf + nd - 1))
    def _():
        pltpu.make_async_copy(h_ref.at[pl.ds(0, tc), :], gbuf, gsem).wait()


def _ffn_tile_order(m, s, nf):
    return jnp.clip(jnp.where(m % 2 == 1, nf - 1 - s, s), 0, nf - 1)


def _ffn(h_packed, idx_flat, gate, mod, w_gate, w_up, w_down, layer, *, cap, mod_row=None):
    n, half = h_packed.shape
    d = 2 * half
    e, r, _ = gate.shape
    ff = w_gate.shape[3]
    tc = min(1024, r if mod_row is not None else cap)
    tf = min(256, ff)
    tn = min(1024, d)
    nf = ff // tf
    nd = d // tn
    assert r % tc == 0 and tc <= n and tc % nd == 0 and (mod_row is not None or cap % tc == 0)
    tiles_per_batch = max(cap // tc, 1)
    mod_off = (N_MOD - 1) * nd
    return pl.pallas_call(
        functools.partial(_ffn_kernel, nf=nf, nd=nd, tf=tf, tc=tc, tiles_per_batch=tiles_per_batch,
                          mod_row=mod_row),
        out_shape=jax.ShapeDtypeStruct((e, r, d), BF16),
        grid_spec=pltpu.PrefetchScalarGridSpec(
            num_scalar_prefetch=1,
            grid=(e, r // tc, nf + nd),
            in_specs=[
                pl.BlockSpec(memory_space=pl.ANY),
                pl.BlockSpec((None, tc, 1), lambda ei, mi, s, idx: (ei, mi, 0)),
                pl.BlockSpec((COND_ROWS, tn), lambda ei, mi, s, idx: (0, mod_off + jnp.maximum(s - nf, 0))),
                pl.BlockSpec((None, None, d, tf),
                             lambda ei, mi, s, idx: (layer, ei, 0, _ffn_tile_order(mi, s, nf))),
                pl.BlockSpec((None, None, d, tf),
                             lambda ei, mi, s, idx: (layer, ei, 0, _ffn_tile_order(mi, s, nf))),
                pl.BlockSpec((None, None, ff, tn),
                             lambda ei, mi, s, idx: (layer, ei, 0, jnp.maximum(s - nf, 0))),
            ],
            out_specs=pl.BlockSpec((None, tc, tn), lambda ei, mi, s, idx: (ei, mi, jnp.maximum(s - nf, 0))),
            scratch_shapes=[pltpu.VMEM((tc, half), jnp.uint32), pltpu.VMEM((tc, half), BF16),
                            pltpu.VMEM((tc, half), BF16), pltpu.VMEM((tc, ff), BF16),
                            pltpu.SemaphoreType.DMA(())],
        ),
        compiler_params=pltpu.CompilerParams(dimension_semantics=("arbitrary",) * 3,
                                             vmem_limit_bytes=V7X_VMEM_LIMIT_BYTES,
                                             disable_bounds_checks=True),
        name="expert_ffn",
    )(idx_flat, h_packed, gate, mod, w_gate, w_up, w_down)


def _combine_kernel(idx_ref, y_ref, x_in_ref, x_ref, buf, gsem, ssem, *, tc):
    del x_in_ref
    t = pl.program_id(0)
    nt = pl.num_programs(0)
    slot = t % 2

    def start_all(chunk, sl, scatter):
        base = chunk * tc

        def body(i, carry):
            hbm = x_ref.at[pl.ds(idx_ref[base + i], 1), :]
            vm = buf.at[sl, pl.ds(i, 1), :]
            if scatter:
                pltpu.make_async_copy(vm, hbm, ssem.at[sl]).start()
            else:
                pltpu.make_async_copy(hbm, vm, gsem.at[sl]).start()
            return carry

        lax.fori_loop(0, tc, body, 0, unroll=8)

    def wait_all(sl, scatter):
        whole = x_ref.at[pl.ds(0, tc), :]
        if scatter:
            pltpu.make_async_copy(buf.at[sl], whole, ssem.at[sl]).wait()
        else:
            pltpu.make_async_copy(whole, buf.at[sl], gsem.at[sl]).wait()

    @pl.when(t == 0)
    def _():
        start_all(0, 0, False)

    @pl.when(t > 0)
    def _():
        wait_all(1 - slot, True)

    @pl.when(t + 1 < nt)
    def _():
        start_all(t + 1, 1 - slot, False)

    wait_all(slot, False)
    buf[slot] = buf[slot] + y_ref[...].astype(F32)
    start_all(t, slot, True)

    @pl.when(t == nt - 1)
    def _():
        wait_all(slot, True)


def _combine(x, y, idx_flat, cap):
    d = x.shape[1]
    e, r, _ = y.shape
    tc = min(512, cap)
    assert cap % tc == 0 and (r // cap >= 2 or cap // tc >= 2)
    nchunk = r // tc
    return pl.pallas_call(
        functools.partial(_combine_kernel, tc=tc),
        out_shape=jax.ShapeDtypeStruct(x.shape, F32),
        grid_spec=pltpu.PrefetchScalarGridSpec(
            num_scalar_prefetch=1,
            grid=(e * nchunk,),
            in_specs=[
                pl.BlockSpec((None, tc, d), lambda t, idx: (t // nchunk, t % nchunk, 0)),
                pl.BlockSpec(memory_space=pl.ANY),
            ],
            out_specs=pl.BlockSpec(memory_space=pl.ANY),
            scratch_shapes=[pltpu.VMEM((2, tc, d), F32), pltpu.SemaphoreType.DMA((2,)),
                            pltpu.SemaphoreType.DMA((2,))],
        ),
        input_output_aliases={2: 0},
        compiler_params=pltpu.CompilerParams(dimension_semantics=("arbitrary",),
                                             vmem_limit_bytes=V7X_VMEM_LIMIT_BYTES,
                                             disable_bounds_checks=True),
        name="moe_combine",
    )(idx_flat, y, x)


def _ec_moe(x, h, aff, mod, w_gate, w_up, w_down, layer, mod_row=None):
    b, n, d = x.shape
    e = aff.shape[1]
    cap = EC_CAPACITY_FACTOR * n // e
    gate, idx = lax.top_k(aff, cap)
    rows = idx + (jnp.arange(b, dtype=jnp.int32) * n)[:, None, None]
    idx_flat = rows.transpose(1, 0, 2).reshape(-1)
    gate_t = gate.transpose(1, 0, 2).reshape(e, b * cap, 1)
    y = _ffn(h.reshape(b * n, d // 2), idx_flat, gate_t, mod, w_gate, w_up, w_down, layer, cap=cap,
             mod_row=mod_row)
    return _combine(x.reshape(b * n, d), y, idx_flat, cap).reshape(b, n, d)


def _rope_tables(n):
    t = np.arange(n, dtype=np.int32)
    row = (t // GRID_W).astype(np.float32)
    col = (t % GRID_W).astype(np.float32)
    pairs = HEAD_DIM // 4
    freq = jnp.asarray(ROPE_BASE, F32) ** (-jnp.arange(pairs, dtype=F32) / pairs)
    ang = jnp.concatenate([jnp.asarray(row)[:, None] * freq, jnp.asarray(col)[:, None] * freq], axis=-1)
    cos, sin = jnp.cos(ang), jnp.sin(ang)
    cos = jnp.stack([cos, cos], axis=-1).reshape(n, HEAD_DIM)
    sin = jnp.stack([-sin, sin], axis=-1).reshape(n, HEAD_DIM)
    return (jnp.stack([cos * Q_SCALE, cos, jnp.ones_like(cos)]),
            jnp.stack([sin * Q_SCALE, sin, jnp.zeros_like(sin)]))


def _scale_table():
    return jnp.broadcast_to(jnp.asarray([Q_SCALE, 1.0, 1.0], F32)[:, None, None], (3, 1, LANES))


def kernel(x, c, ctx, c_ctx, ada_w, ada_b, norm1_g, norm2_g, na_w_qkv, na_w_o, na_rpb, da_w_qkv, da_w_o, da_lambda_q1, da_lambda_k1, da_lambda_q2, da_lambda_k2, da_subln_g, moe_w_router, moe_w_gate, moe_w_up, moe_w_down, final_g):
    b, n, d = x.shape
    n_ctx = ctx.shape[1]
    depth = ada_w.shape[0]
    rows = n // GRID_W
    assert n % GRID_W == 0 and rows % NA_ROW_BLOCK == 0 and rows // NA_ROW_BLOCK >= 3
    assert b + 1 <= COND_ROWS

    cond = jnp.zeros((COND_ROWS, d), F32).at[:b].set(c).at[b].set(c_ctx)
    ada_b3 = ada_b.reshape(depth, 1, N_MOD * d)
    rope = _rope_tables(n)
    plain = _scale_table()

    for i in range(depth):
        last = i == depth - 1
        j = i // N_MIXERS
        mod = _adaln(cond, ada_w, ada_b3, i)
        h_l = _prep(x, norm1_g[i], mod, shift_chunk=0, scale_chunk=1).reshape(b * n, d)
        h_c = _prep(ctx, norm1_g[i], mod, shift_chunk=0, scale_chunk=1, mod_row=b).reshape(b * n_ctx, d)
        if i % N_MIXERS == 0:
            w_o = na_w_o
            qkv_l = _matmul(h_l, na_w_qkv, j, mode="qkv", tables=plain).reshape(b, n, 3 * d)
            qkv_c = _matmul(h_c, na_w_qkv, j, mode="qkv", tables=plain).reshape(b, n_ctx, 3 * d)
            o_l = _natten(qkv_l, qkv_c, _natten_bias(na_rpb[j], rows))
            o_c = None if last else _ctx_attn(qkv_c)
        else:
            w_o = da_w_o
            qkv_l = _matmul(h_l, da_w_qkv, j, mode="qkv", tables=rope, rows_per_batch=n).reshape(b, n, 3 * d)
            qkv_c = _matmul(h_c, da_w_qkv, j, mode="qkv", tables=plain).reshape(b, n_ctx, 3 * d)
            o_l = _diff_attn(qkv_l, qkv_c, da_lambda_q1[j], da_lambda_k1[j], da_lambda_q2[j],
                             da_lambda_k2[j], da_subln_g[j], _lambda_init(i))
            assert last, "context update after a differential layer is not needed at this depth"
            o_c = None
        x = _matmul(o_l.reshape(b * n, d), w_o, j, mode="resid", out_dtype=F32, resid=x.reshape(b * n, d),
                    mod=mod, gate_chunk=2, rows_per_batch=n).reshape(b, n, d)
        router_t = moe_w_router[i].T
        h2, aff = _prep(x, norm2_g[i], mod, shift_chunk=3, scale_chunk=4, router_t=router_t,
                        out_dtype=jnp.uint32)
        x = _ec_moe(x, h2, aff, mod, moe_w_gate, moe_w_up, moe_w_down, i)
        if not last:
            ctx = _matmul(o_c.reshape(b * n_ctx, d), w_o, j, mode="resid", out_dtype=F32,
                          resid=ctx.reshape(b * n_ctx, d), mod=mod, gate_chunk=2,
                          rows_per_batch=n_ctx, mod_row=b).reshape(b, n_ctx, d)
            h2c, affc = _prep(ctx, norm2_g[i], mod, shift_chunk=3, scale_chunk=4, mod_row=b,
                              router_t=router_t, out_dtype=jnp.uint32)
            ctx = _ec_moe(ctx, h2c, affc, mod, moe_w_gate, moe_w_up, moe_w_down, i, mod_row=b)
    return _prep(x, final_g, None, out_dtype=F32)
```

```python
import functools
import math

import numpy as np
import jax
import jax.numpy as jnp
from jax import lax
from jax.experimental import pallas as pl
from jax.experimental.pallas import tpu as pltpu

F32 = jnp.float32
BF16 = jnp.bfloat16

GRID_W = 64
WIN_ROWS = 8
WIN_COLS = 16
HEAD_DIM = 128
N_MIXERS = 2
N_MOD = 6
ROPE_BASE = 10000.0
EC_CAPACITY_FACTOR = 2
NORM_EPS = 1e-6
SUBLN_EPS = 1e-5
NEG_INF = -1e30
LOG2E = math.log2(math.e)
Q_SCALE = HEAD_DIM ** -0.5 * LOG2E

V7X_VMEM_LIMIT_BYTES = 56 * 1024 * 1024
LANES = 128
COND_ROWS = 8
NA_ROW_BLOCK = 4
NA_KEY_ROWS = NA_ROW_BLOCK + WIN_ROWS - 1
NA_HEADS_PER_STEP = 4
DA_Q_TILE = 1024
DA_K_TILE = 512
DA_UNROLL = 8
MM_ROW_CHUNK = 256


def _params(*sem):
    return pltpu.CompilerParams(dimension_semantics=sem, vmem_limit_bytes=V7X_VMEM_LIMIT_BYTES)


def _lambda_init(layer_idx):
    return 0.8 - 0.6 * math.exp(-0.3 * layer_idx)


def _adaln_kernel(s_ref, w_ref, b_ref, o_ref):
    s = s_ref[...]
    s = s * (1.0 / (1.0 + jnp.exp(-s)))
    acc = jnp.dot(s.astype(BF16), w_ref[...].astype(BF16), preferred_element_type=F32)
    o_ref[...] = acc + b_ref[...]


def _adaln(cond, ada_w, ada_b3, layer):
    d = cond.shape[1]
    n6 = ada_w.shape[2]
    tn = min(512, n6)
    return pl.pallas_call(
        _adaln_kernel,
        out_shape=jax.ShapeDtypeStruct((COND_ROWS, n6), F32),
        grid=(n6 // tn,),
        in_specs=[
            pl.BlockSpec((COND_ROWS, d), lambda j: (0, 0)),
            pl.BlockSpec((None, d, tn), lambda j: (layer, 0, j)),
            pl.BlockSpec((None, 1, tn), lambda j: (layer, 0, j)),
        ],
        out_specs=pl.BlockSpec((COND_ROWS, tn), lambda j: (0, j)),
        compiler_params=_params("arbitrary"),
        name="adaln",
    )(cond, ada_w, ada_b3)


def _prep_kernel(x_ref, g_ref, *rest, eps, mod_row, modulate, with_router):
    rest = list(rest)
    if modulate:
        sh_ref, sc_ref = rest[:2]
        rest = rest[2:]
    if with_router:
        wr_ref, o_ref, aff_ref = rest
    else:
        (o_ref,) = rest
    x = x_ref[...]
    h = x * lax.rsqrt(jnp.mean(x * x, axis=-1, keepdims=True) + eps) * g_ref[...]
    if modulate:
        row = pl.program_id(0) if mod_row is None else mod_row
        h = h * (1.0 + sc_ref[pl.ds(row, 1), :]) + sh_ref[pl.ds(row, 1), :]
    if o_ref.dtype == jnp.uint32:
        bits = pltpu.bitcast(h.astype(BF16).astype(F32), jnp.uint32)
        half = o_ref.shape[1]
        o_ref[...] = (bits[:, :half] >> 16) | bits[:, half:]
    else:
        o_ref[...] = h.astype(o_ref.dtype)
    if with_router:
        logits = lax.dot_general(wr_ref[...], h, (((1,), (1,)), ((), ())),
                                 precision=lax.Precision.HIGHEST, preferred_element_type=F32)
        e = jnp.exp(logits - jnp.max(logits, axis=0, keepdims=True))
        aff_ref[...] = e / jnp.sum(e, axis=0, keepdims=True)


def _prep(x, g, mod, *, shift_chunk=None, scale_chunk=None, mod_row=None, out_dtype=BF16,
          router_t=None, eps=NORM_EPS):
    b, n, d = x.shape
    tm = min(256, n)
    modulate = shift_chunk is not None
    with_router = router_t is not None
    in_specs = [pl.BlockSpec((None, tm, d), lambda bi, i: (bi, i, 0)),
                pl.BlockSpec((1, d), lambda bi, i: (0, 0))]
    args = [x, g.reshape(1, d)]
    if modulate:
        in_specs += [pl.BlockSpec((COND_ROWS, d), lambda bi, i: (0, shift_chunk)),
                     pl.BlockSpec((COND_ROWS, d), lambda bi, i: (0, scale_chunk))]
        args += [mod, mod]
    d_out = d // 2 if out_dtype == jnp.uint32 else d
    out_shape = [jax.ShapeDtypeStruct((b, n, d_out), out_dtype)]
    out_specs = [pl.BlockSpec((None, tm, d_out), lambda bi, i: (bi, i, 0))]
    if with_router:
        e = router_t.shape[0]
        in_specs.append(pl.BlockSpec((e, d), lambda bi, i: (0, 0)))
        args.append(router_t)
        out_shape.append(jax.ShapeDtypeStruct((b, e, n), F32))
        out_specs.append(pl.BlockSpec((None, e, tm), lambda bi, i: (bi, 0, i)))
    out = pl.pallas_call(
        functools.partial(_prep_kernel, eps=eps, mod_row=mod_row, modulate=modulate,
                          with_router=with_router),
        out_shape=out_shape,
        grid=(b, n // tm),
        in_specs=in_specs,
        out_specs=out_specs,
        compiler_params=_params("arbitrary", "arbitrary"),
        name="prep",
    )(*args)
    return out if with_router else out[0]


def _mm_kernel(a_ref, w_ref, *rest, mode, rope, tiles_per_batch, mod_row):
    o_ref, wb_ref = rest[-2:]
    i = pl.program_id(1)

    @pl.when(i == 0)
    def _():
        wb_ref[...] = w_ref[...].astype(BF16)

    if mode == "resid":
        acc = jnp.dot(a_ref[...], wb_ref[...], preferred_element_type=F32)
        x_ref, gate_ref = rest[:2]
        row = i // tiles_per_batch if mod_row is None else mod_row
        o_ref[...] = x_ref[...] + gate_ref[pl.ds(row, 1), :] * acc
        return

    chunk = min(MM_ROW_CHUNK, o_ref.shape[0])
    if rope:
        even = lax.broadcasted_iota(jnp.int32, (chunk, LANES), 1) % 2 == 0
    else:
        scale = rest[0][...]
    for r in range(o_ref.shape[0] // chunk):
        rows = slice(r * chunk, (r + 1) * chunk)
        acc = jnp.dot(a_ref[rows, :], wb_ref[...], preferred_element_type=F32)
        if rope:
            cos = rest[0][rows, :]
            sin = rest[1][rows, :]
        for c in range(o_ref.shape[1] // LANES):
            xc = acc[:, c * LANES:(c + 1) * LANES]
            if rope:
                partner = jnp.where(even, pltpu.roll(xc, LANES - 1, 1), pltpu.roll(xc, 1, 1))
                xc = xc * cos + partner * sin
            else:
                xc = xc * scale
            o_ref[rows, c * LANES:(c + 1) * LANES] = xc.astype(o_ref.dtype)


def _matmul(a, w, layer, *, mode, out_dtype=BF16, tables=None, resid=None, mod=None, gate_chunk=None,
            rows_per_batch=None, mod_row=None):
    m, k = a.shape
    n = w.shape[2]
    tm = min(1024, m)
    tn = min(512, n)
    if rows_per_batch is not None:
        tm = min(tm, rows_per_batch)
    in_specs = [pl.BlockSpec((tm, k), lambda j, i: (i, 0)),
                pl.BlockSpec((None, k, tn), lambda j, i: (layer, 0, j))]
    args = [a, w]
    tiles_per_batch = 1
    rope = isinstance(tables, tuple)
    if mode == "qkv":
        nq_tiles = n // 3 // tn
        if rope:
            nblk = tables[0].shape[1] // tm
            spec = pl.BlockSpec((None, tm, LANES), lambda j, i: (j // nq_tiles, i % nblk, 0))
            in_specs += [spec, spec]
            args += list(tables)
        else:
            in_specs.append(pl.BlockSpec((None, 1, LANES), lambda j, i: (j // nq_tiles, 0, 0)))
            args.append(tables)
    else:
        tiles_per_batch = rows_per_batch // tm
        gate_off = gate_chunk * (n // tn)
        in_specs += [pl.BlockSpec((tm, tn), lambda j, i: (i, j)),
                     pl.BlockSpec((COND_ROWS, tn), lambda j, i: (0, gate_off + j))]
        args += [resid, mod]
    return pl.pallas_call(
        functools.partial(_mm_kernel, mode=mode, rope=rope, tiles_per_batch=tiles_per_batch,
                          mod_row=mod_row),
        out_shape=jax.ShapeDtypeStruct((m, n), out_dtype),
        grid=(n // tn, m // tm),
        in_specs=in_specs,
        out_specs=pl.BlockSpec((tm, tn), lambda j, i: (i, j)),
        scratch_shapes=[pltpu.VMEM((k, tn), BF16)],
        compiler_params=_params("arbitrary", "arbitrary"),
        name="matmul_" + mode,
    )(*args)


def _natten_kernel(blk_ref, q_ref, k_ref, v_ref, kc_ref, vc_ref, tab_ref, o_ref, bias_ref, *, rows):
    r = pl.program_id(2)
    nblk = pl.num_programs(2)
    ws = jnp.clip(r * NA_ROW_BLOCK - WIN_ROWS // 2, 0, rows - NA_KEY_ROWS)
    start = pl.multiple_of(ws * GRID_W, GRID_W)
    nk = NA_KEY_ROWS * GRID_W
    nt = (((1,), (1,)), ((), ()))

    @pl.when((r <= 1) | (r == nblk - 1))
    def _():
        cls = jnp.where(r == 0, 0, jnp.where(r == nblk - 1, 2, 1))
        for h in range(NA_HEADS_PER_STEP):
            for a in range(NA_ROW_BLOCK):
                for kr in range(NA_KEY_ROWS):
                    blk = blk_ref[(cls * NA_ROW_BLOCK + a) * NA_KEY_ROWS + kr]
                    bias_ref[h, a * GRID_W:(a + 1) * GRID_W, kr * GRID_W:(kr + 1) * GRID_W] = tab_ref[h, blk]

    for h in range(NA_HEADS_PER_STEP):
        sl = slice(h * HEAD_DIM, (h + 1) * HEAD_DIM)
        q = q_ref[:, sl]
        k = k_ref[pl.ds(start, nk), sl]
        v = v_ref[pl.ds(start, nk), sl]
        s = lax.dot_general(q, k, nt, preferred_element_type=F32) + bias_ref[h]
        sc = lax.dot_general(q, kc_ref[:, sl], nt, preferred_element_type=F32)
        m = jnp.maximum(jnp.max(s, axis=-1, keepdims=True), jnp.max(sc, axis=-1, keepdims=True))
        p = jnp.exp2(s - m)
        pc = jnp.exp2(sc - m)
        vc = vc_ref[:, sl]
        ones = jnp.ones((nk, HEAD_DIM), BF16)
        oa = (jnp.dot(p.astype(BF16), jnp.concatenate([v, ones], axis=1), preferred_element_type=F32)
              + jnp.dot(pc.astype(BF16), jnp.concatenate([vc, ones[:vc.shape[0]]], axis=1),
                        preferred_element_type=F32))
        o_ref[:, sl] = (oa[:, :HEAD_DIM] / oa[:, HEAD_DIM:]).astype(o_ref.dtype)


def _natten_bias(rpb, rows):
    nblk = rows // NA_ROW_BLOCK
    kh = WIN_ROWS
    qc = np.arange(GRID_W, dtype=np.int32)
    col_start = np.clip(qc - WIN_COLS // 2, 0, GRID_W - WIN_COLS)
    col_mask = (qc[None, :] >= col_start[:, None]) & (qc[None, :] < col_start[:, None] + WIN_COLS)
    col_idx = np.clip(qc[None, :] - qc[:, None] + WIN_COLS - 1, 0, 2 * WIN_COLS - 2).astype(np.int32)
    row_idx = np.zeros((3, NA_ROW_BLOCK, NA_KEY_ROWS), np.int32)
    valid = np.zeros((3, NA_ROW_BLOCK, NA_KEY_ROWS), bool)
    for cls, blk in enumerate((0, 1, nblk - 1)):
        r0 = blk * NA_ROW_BLOCK
        ws = int(np.clip(r0 - WIN_ROWS // 2, 0, rows - NA_KEY_ROWS))
        for a in range(NA_ROW_BLOCK):
            r = r0 + a
            rs = int(np.clip(r - kh // 2, 0, rows - kh))
            for kr in range(NA_KEY_ROWS):
                key_row = ws + kr
                if rs <= key_row < rs + kh:
                    valid[cls, a, kr] = True
                    row_idx[cls, a, kr] = key_row - r + (WIN_ROWS - 1)
    for blk in range(1, nblk - 1):
        r0 = blk * NA_ROW_BLOCK
        ws = int(np.clip(r0 - WIN_ROWS // 2, 0, rows - NA_KEY_ROWS))
        for a in range(NA_ROW_BLOCK):
            rs = int(np.clip(r0 + a - kh // 2, 0, rows - kh))
            assert rs - ws == a and r0 - ws == WIN_ROWS // 2
    outside = 2 * WIN_ROWS - 1
    blk_ids = np.where(valid, row_idx, outside).astype(np.int32).reshape(-1)
    t = jnp.where(col_mask[None, None], rpb[:, :, col_idx].astype(F32) * LOG2E, NEG_INF)
    t = jnp.concatenate([t, jnp.full((rpb.shape[0], 1, GRID_W, GRID_W), NEG_INF, F32)], axis=1)
    return jnp.asarray(blk_ids), t


def _natten(qkv_l, qkv_c, bias):
    blk_ids, tab = bias
    b, n, d3 = qkv_l.shape
    d = d3 // 3
    hw = NA_HEADS_PER_STEP * HEAD_DIM
    ng = d // hw
    ctx = qkv_c.shape[1]
    rows = n // GRID_W
    nblk = rows // NA_ROW_BLOCK
    tq = NA_ROW_BLOCK * GRID_W
    nk = NA_KEY_ROWS * GRID_W

    hps = NA_HEADS_PER_STEP
    return pl.pallas_call(
        functools.partial(_natten_kernel, rows=rows),
        out_shape=jax.ShapeDtypeStruct((b, n, d), BF16),
        grid_spec=pltpu.PrefetchScalarGridSpec(
            num_scalar_prefetch=1,
            grid=(b, ng, nblk),
            in_specs=[
                pl.BlockSpec((None, tq, hw), lambda bi, h, r, ids: (bi, r, h)),
                pl.BlockSpec((None, n, hw), lambda bi, h, r, ids: (bi, 0, ng + h)),
                pl.BlockSpec((None, n, hw), lambda bi, h, r, ids: (bi, 0, 2 * ng + h)),
                pl.BlockSpec((None, ctx, hw), lambda bi, h, r, ids: (bi, 0, ng + h)),
                pl.BlockSpec((None, ctx, hw), lambda bi, h, r, ids: (bi, 0, 2 * ng + h)),
                pl.BlockSpec((hps, 2 * WIN_ROWS, GRID_W, GRID_W), lambda bi, h, r, ids: (h, 0, 0, 0)),
            ],
            out_specs=pl.BlockSpec((None, tq, hw), lambda bi, h, r, ids: (bi, r, h)),
            scratch_shapes=[pltpu.VMEM((hps, tq, nk), F32)],
        ),
        compiler_params=_params("arbitrary", "arbitrary", "arbitrary"),
        name="natten",
    )(blk_ids, qkv_l, qkv_l, qkv_l, qkv_c, qkv_c, tab)


def _ctx_attn_kernel(q_ref, k_ref, v_ref, o_ref):
    s = lax.dot_general(q_ref[...], k_ref[...], (((1,), (1,)), ((), ())), preferred_element_type=F32)
    p = jnp.exp2(s - jnp.max(s, axis=-1, keepdims=True))
    l = jnp.sum(p, axis=-1, keepdims=True)
    o = jnp.dot(p.astype(BF16), v_ref[...], preferred_element_type=F32)
    o_ref[...] = (o / l).astype(o_ref.dtype)


def _ctx_attn(qkv_c):
    b, ctx, d3 = qkv_c.shape
    d = d3 // 3
    nh = d // HEAD_DIM
    return pl.pallas_call(
        _ctx_attn_kernel,
        out_shape=jax.ShapeDtypeStruct((b, ctx, d), BF16),
        grid=(b, nh),
        in_specs=[
            pl.BlockSpec((None, ctx, HEAD_DIM), lambda bi, h: (bi, 0, h)),
            pl.BlockSpec((None, ctx, HEAD_DIM), lambda bi, h: (bi, 0, nh + h)),
            pl.BlockSpec((None, ctx, HEAD_DIM), lambda bi, h: (bi, 0, 2 * nh + h)),
        ],
        out_specs=pl.BlockSpec((None, ctx, HEAD_DIM), lambda bi, h: (bi, 0, h)),
        compiler_params=_params("arbitrary", "arbitrary"),
        name="ctx_attn",
    )(qkv_c, qkv_c, qkv_c)


def _lane_tile(x, width):
    return jnp.concatenate([x] * (width // LANES), axis=1)


def _diff_kernel(q_ref, k_ref, v_ref, kc_ref, vc_ref, lq1_ref, lk1_ref, lq2_ref, lk2_ref, g_ref,
                 o_ref, m1, m2, l1, l2, a1, a2, *, tk, unroll, lambda_init):
    n = k_ref.shape[0]
    nt = (((1,), (1,)), ((), ()))
    ms, ls, accs = (m1, m2), (l1, l2), (a1, a2)
    for c in range(2):
        ms[c][...] = jnp.full(ms[c].shape, NEG_INF, F32)
        ls[c][...] = jnp.zeros(ls[c].shape, F32)
        accs[c][...] = jnp.zeros(accs[c].shape, F32)

    def step(k_blk, v_blk):
        for c in range(2):
            q = q_ref[:, c * HEAD_DIM:(c + 1) * HEAD_DIM]
            s = lax.dot_general(q, k_blk[:, c * HEAD_DIM:(c + 1) * HEAD_DIM], nt,
                                preferred_element_type=F32)
            m_prev = ms[c][...]
            m_new = jnp.maximum(m_prev, jnp.max(s, axis=-1, keepdims=True))
            alpha = jnp.exp2(m_prev - m_new)
            p = jnp.exp2(s - _lane_tile(m_new, s.shape[1]))
            ls[c][...] = alpha * ls[c][...] + jnp.sum(p, axis=-1, keepdims=True)
            pv = jnp.dot(p.astype(BF16), v_blk, preferred_element_type=F32)
            accs[c][...] = _lane_tile(alpha, pv.shape[1]) * accs[c][...] + pv
            ms[c][...] = m_new

    def body(j, carry):
        st = pl.multiple_of(j * tk, tk)
        step(k_ref[pl.ds(st, tk), :], v_ref[pl.ds(st, tk), :])
        return carry

    lax.fori_loop(0, n // tk, body, 0, unroll=unroll)
    step(kc_ref[...], vc_ref[...])

    lam = (jnp.exp(jnp.sum(lq1_ref[...] * lk1_ref[...], axis=-1, keepdims=True))
           - jnp.exp(jnp.sum(lq2_ref[...] * lk2_ref[...], axis=-1, keepdims=True)) + lambda_init)
    w = a1.shape[1]
    o = a1[...] / _lane_tile(l1[...], w) - lam * (a2[...] / _lane_tile(l2[...], w))
    o = o * lax.rsqrt(jnp.mean(o * o, axis=-1, keepdims=True) + SUBLN_EPS) * g_ref[...]
    o_ref[...] = (o * (1.0 - lambda_init)).astype(o_ref.dtype)


def _diff_attn(qkv_l, qkv_c, lq1, lk1, lq2, lk2, subln_g, lambda_init):
    b, n, d3 = qkv_l.shape
    d = d3 // 3
    hw = 2 * HEAD_DIM
    nh = d // hw
    ctx = qkv_c.shape[1]
    tq = min(DA_Q_TILE, n)
    tk = min(DA_K_TILE, n)
    vec = lambda a: a.reshape(1, -1).astype(F32)
    small = pl.BlockSpec((1, HEAD_DIM), lambda bi, h, i: (0, 0))
    return pl.pallas_call(
        functools.partial(_diff_kernel, tk=tk, unroll=min(DA_UNROLL, n // tk), lambda_init=lambda_init),
        out_shape=jax.ShapeDtypeStruct((b, n, d), BF16),
        grid=(b, nh, n // tq),
        in_specs=[
            pl.BlockSpec((None, tq, hw), lambda bi, h, i: (bi, i, h)),
            pl.BlockSpec((None, n, hw), lambda bi, h, i: (bi, 0, nh + h)),
            pl.BlockSpec((None, n, hw), lambda bi, h, i: (bi, 0, 2 * nh + h)),
            pl.BlockSpec((None, ctx, hw), lambda bi, h, i: (bi, 0, nh + h)),
            pl.BlockSpec((None, ctx, hw), lambda bi, h, i: (bi, 0, 2 * nh + h)),
            small, small, small, small,
            pl.BlockSpec((1, hw), lambda bi, h, i: (0, 0)),
        ],
        out_specs=pl.BlockSpec((None, tq, hw), lambda bi, h, i: (bi, i, h)),
        scratch_shapes=[pltpu.VMEM((tq, LANES), F32)] * 4 + [pltpu.VMEM((tq, hw), F32)] * 2,
        compiler_params=_params("arbitrary", "arbitrary", "arbitrary"),
        name="diff_attn",
    )(qkv_l, qkv_l, qkv_l, qkv_c, qkv_c, vec(lq1), vec(lk1), vec(lq2), vec(lk2), vec(subln_g))


def _ffn_kernel(idx_ref, h_ref, gate_ref, mod_ref, wg_ref, wu_ref, wd_ref, o_ref,
                gbuf, xlo, xhi, a_ref, gsem, *, nf, nd, tf, tc, tiles_per_batch, mod_row):
    m = pl.program_id(1)
    s = pl.program_id(2)
    nm = pl.num_programs(1)
    tile = pl.program_id(0) * nm + m
    ntiles = pl.num_programs(0) * nm

    def start_gather(tl):
        base = tl * tc

        def body(i, carry):
            src = h_ref.at[pl.ds(idx_ref[base + i], 1), :]
            pltpu.make_async_copy(src, gbuf.at[pl.ds(i, 1), :], gsem).start()
            return carry

        lax.fori_loop(0, tc, body, 0, unroll=8)

    nsteps = nf + nd
    first_issue_step = 0 if tc % nsteps == 0 else nf
    per = tc // (nsteps - first_issue_step)

    def request_slice():
        off = (s - first_issue_step) * per
        base = jnp.minimum(tile + 1, ntiles - 1) * tc + off
        for i in range(per):
            src = h_ref.at[pl.ds(idx_ref[base + i], 1), :]
            pltpu.make_async_copy(src, gbuf.at[pl.ds(off + i, 1), :], gsem).start()

    @pl.when((tile == 0) & (s == 0))
    def _():
        start_gather(0)

    @pl.when(s == 0)
    def _():
        pltpu.make_async_copy(h_ref.at[pl.ds(0, tc), :], gbuf, gsem).wait()
        w = gbuf[...]
        xlo[...] = pltpu.bitcast(w << 16, F32).astype(BF16)
        xhi[...] = pltpu.bitcast(w & jnp.uint32(0xFFFF0000), F32).astype(BF16)

    @pl.when(s < nf)
    def _():
        half = xlo.shape[1]
        wg = wg_ref[...].astype(BF16)
        wu = wu_ref[...].astype(BF16)
        g = (jnp.dot(xlo[...], wg[:half], preferred_element_type=F32)
             + jnp.dot(xhi[...], wg[half:], preferred_element_type=F32))
        u = (jnp.dot(xlo[...], wu[:half], preferred_element_type=F32)
             + jnp.dot(xhi[...], wu[half:], preferred_element_type=F32))
        if first_issue_step == 0:
            request_slice()
        a = (g * (1.0 / (1.0 + jnp.exp(-g))) * u).astype(BF16)
        fe = _ffn_tile_order(m, s, nf)
        for f in range(nf):
            @pl.when(fe == f)
            def _():
                a_ref[:, f * tf:(f + 1) * tf] = a

    @pl.when(s >= nf)
    def _():
        request_slice()
        y = jnp.dot(a_ref[...], wd_ref[...].astype(BF16), preferred_element_type=F32)
        row = m // tiles_per_batch if mod_row is None else mod_row
        o_ref[...] = (y * gate_ref[...] * mod_ref[pl.ds(row, 1), :]).astype(o_ref.dtype)

    @pl.when((tile == ntiles - 1) & (s == nf + nd - 1))
    def _():
        pltpu.make_async_copy(h_ref.at[pl.ds(0, tc), :], gbuf, gsem).wait()


def _ffn_tile_order(m, s, nf):
    return jnp.clip(jnp.where(m % 2 == 1, nf - 1 - s, s), 0, nf - 1)


def _ffn(h_packed, idx_flat, gate, mod, w_gate, w_up, w_down, layer, *, cap, mod_row=None):
    n, half = h_packed.shape
    d = 2 * half
    e, r, _ = gate.shape
    ff = w_gate.shape[3]
    tc = min(1024, r if mod_row is not None else cap)
    tf = min(256, ff)
    tn = min(1024, d)
    nf = ff // tf
    nd = d // tn
    assert r % tc == 0 and tc <= n and tc % nd == 0 and (mod_row is not None or cap % tc == 0)
    tiles_per_batch = max(cap // tc, 1)
    mod_off = (N_MOD - 1) * nd
    return pl.pallas_call(
        functools.partial(_ffn_kernel, nf=nf, nd=nd, tf=tf, tc=tc, tiles_per_batch=tiles_per_batch,
                          mod_row=mod_row),
        out_shape=jax.ShapeDtypeStruct((e, r, d), BF16),
        grid_spec=pltpu.PrefetchScalarGridSpec(
            num_scalar_prefetch=1,
            grid=(e, r // tc, nf + nd),
            in_specs=[
                pl.BlockSpec(memory_space=pl.ANY),
                pl.BlockSpec((None, tc, 1), lambda ei, mi, s, idx: (ei, mi, 0)),
                pl.BlockSpec((COND_ROWS, tn), lambda ei, mi, s, idx: (0, mod_off + jnp.maximum(s - nf, 0))),
                pl.BlockSpec((None, None, d, tf),
                             lambda ei, mi, s, idx: (layer, ei, 0, _ffn_tile_order(mi, s, nf))),
                pl.BlockSpec((None, None, d, tf),
                             lambda ei, mi, s, idx: (layer, ei, 0, _ffn_tile_order(mi, s, nf))),
                pl.BlockSpec((None, None, ff, tn),
                             lambda ei, mi, s, idx: (layer, ei, 0, jnp.maximum(s - nf, 0))),
            ],
            out_specs=pl.BlockSpec((None, tc, tn), lambda ei, mi, s, idx: (ei, mi, jnp.maximum(s - nf, 0))),
            scratch_shapes=[pltpu.VMEM((tc, half), jnp.uint32), pltpu.VMEM((tc, half), BF16),
                            pltpu.VMEM((tc, half), BF16), pltpu.VMEM((tc, ff), BF16),
                            pltpu.SemaphoreType.DMA(())],
        ),
        compiler_params=pltpu.CompilerParams(dimension_semantics=("arbitrary",) * 3,
                                             vmem_limit_bytes=V7X_VMEM_LIMIT_BYTES,
                                             disable_bounds_checks=True),
        name="expert_ffn",
    )(idx_flat, h_packed, gate, mod, w_gate, w_up, w_down)


def _combine_kernel(idx_ref, y_ref, x_in_ref, x_ref, buf, gsem, ssem, *, tc):
    del x_in_ref
    t = pl.program_id(0)
    nt = pl.num_programs(0)
    slot = t % 2

    def start_all(chunk, sl, scatter, straight_line=True):
        base = chunk * tc

        def body(i, carry):
            hbm = x_ref.at[pl.ds(idx_ref[base + i], 1), :]
            vm = buf.at[sl, pl.ds(i, 1), :]
            if scatter:
                pltpu.make_async_copy(vm, hbm, ssem.at[sl]).start()
            else:
                pltpu.make_async_copy(hbm, vm, gsem.at[sl]).start()
            return carry

        if straight_line:
            for i in range(tc):
                body(i, 0)
        else:
            lax.fori_loop(0, tc, body, 0, unroll=8)

    def wait_all(sl, scatter):
        whole = x_ref.at[pl.ds(0, tc), :]
        if scatter:
            pltpu.make_async_copy(buf.at[sl], whole, ssem.at[sl]).wait()
        else:
            pltpu.make_async_copy(whole, buf.at[sl], gsem.at[sl]).wait()

    @pl.when(t == 0)
    def _():
        start_all(0, 0, False, straight_line=False)

    @pl.when(t > 0)
    def _():
        wait_all(1 - slot, True)

    @pl.when(t + 1 < nt)
    def _():
        start_all(t + 1, 1 - slot, False)

    wait_all(slot, False)
    buf[slot] = buf[slot] + y_ref[...].astype(F32)
    start_all(t, slot, True)

    @pl.when(t == nt - 1)
    def _():
        wait_all(slot, True)


def _combine(x, y, idx_flat, cap):
    d = x.shape[1]
    e, r, _ = y.shape
    tc = min(512, cap)
    assert cap % tc == 0 and (r // cap >= 2 or cap // tc >= 2)
    nchunk = r // tc
    return pl.pallas_call(
        functools.partial(_combine_kernel, tc=tc),
        out_shape=jax.ShapeDtypeStruct(x.shape, F32),
        grid_spec=pltpu.PrefetchScalarGridSpec(
            num_scalar_prefetch=1,
            grid=(e * nchunk,),
            in_specs=[
                pl.BlockSpec((None, tc, d), lambda t, idx: (t // nchunk, t % nchunk, 0)),
                pl.BlockSpec(memory_space=pl.ANY),
            ],
            out_specs=pl.BlockSpec(memory_space=pl.ANY),
            scratch_shapes=[pltpu.VMEM((2, tc, d), F32), pltpu.SemaphoreType.DMA((2,)),
                            pltpu.SemaphoreType.DMA((2,))],
        ),
        input_output_aliases={2: 0},
        compiler_params=pltpu.CompilerParams(dimension_semantics=("arbitrary",),
                                             vmem_limit_bytes=V7X_VMEM_LIMIT_BYTES,
                                             disable_bounds_checks=True),
        name="moe_combine",
    )(idx_flat, y, x)


def _ec_moe(x, h, aff, mod, w_gate, w_up, w_down, layer, mod_row=None):
    b, n, d = x.shape
    e = aff.shape[1]
    cap = EC_CAPACITY_FACTOR * n // e
    gate, idx = lax.top_k(aff, cap)
    rows = idx + (jnp.arange(b, dtype=jnp.int32) * n)[:, None, None]
    idx_flat = rows.transpose(1, 0, 2).reshape(-1)
    gate_t = gate.transpose(1, 0, 2).reshape(e, b * cap, 1)
    y = _ffn(h.reshape(b * n, d // 2), idx_flat, gate_t, mod, w_gate, w_up, w_down, layer, cap=cap,
             mod_row=mod_row)
    return _combine(x.reshape(b * n, d), y, idx_flat, cap).reshape(b, n, d)


def _rope_tables(n):
    t = np.arange(n, dtype=np.int32)
    row = (t // GRID_W).astype(np.float32)
    col = (t % GRID_W).astype(np.float32)
    pairs = HEAD_DIM // 4
    freq = jnp.asarray(ROPE_BASE, F32) ** (-jnp.arange(pairs, dtype=F32) / pairs)
    ang = jnp.concatenate([jnp.asarray(row)[:, None] * freq, jnp.asarray(col)[:, None] * freq], axis=-1)
    cos, sin = jnp.cos(ang), jnp.sin(ang)
    cos = jnp.stack([cos, cos], axis=-1).reshape(n, HEAD_DIM)
    sin = jnp.stack([-sin, sin], axis=-1).reshape(n, HEAD_DIM)
    return (jnp.stack([cos * Q_SCALE, cos, jnp.ones_like(cos)]),
            jnp.stack([sin * Q_SCALE, sin, jnp.zeros_like(sin)]))


def _scale_table():
    return jnp.broadcast_to(jnp.asarray([Q_SCALE, 1.0, 1.0], F32)[:, None, None], (3, 1, LANES))


def kernel(x, c, ctx, c_ctx, ada_w, ada_b, norm1_g, norm2_g, na_w_qkv, na_w_o, na_rpb, da_w_qkv, da_w_o, da_lambda_q1, da_lambda_k1, da_lambda_q2, da_lambda_k2, da_subln_g, moe_w_router, moe_w_gate, moe_w_up, moe_w_down, final_g):
    b, n, d = x.shape
    n_ctx = ctx.shape[1]
    depth = ada_w.shape[0]
    rows = n // GRID_W
    assert n % GRID_W == 0 and rows % NA_ROW_BLOCK == 0 and rows // NA_ROW_BLOCK >= 3
    assert b + 1 <= COND_ROWS

    cond = jnp.zeros((COND_ROWS, d), F32).at[:b].set(c).at[b].set(c_ctx)
    ada_b3 = ada_b.reshape(depth, 1, N_MOD * d)
    rope = _rope_tables(n)
    plain = _scale_table()

    for i in range(depth):
        last = i == depth - 1
        j = i // N_MIXERS
        mod = _adaln(cond, ada_w, ada_b3, i)
        h_l = _prep(x, norm1_g[i], mod, shift_chunk=0, scale_chunk=1).reshape(b * n, d)
        h_c = _prep(ctx, norm1_g[i], mod, shift_chunk=0, scale_chunk=1, mod_row=b).reshape(b * n_ctx, d)
        if i % N_MIXERS == 0:
            w_o = na_w_o
            qkv_l = _matmul(h_l, na_w_qkv, j, mode="qkv", tables=plain).reshape(b, n, 3 * d)
            qkv_c = _matmul(h_c, na_w_qkv, j, mode="qkv", tables=plain).reshape(b, n_ctx, 3 * d)
            o_l = _natten(qkv_l, qkv_c, _natten_bias(na_rpb[j], rows))
            o_c = None if last else _ctx_attn(qkv_c)
        else:
            w_o = da_w_o
            qkv_l = _matmul(h_l, da_w_qkv, j, mode="qkv", tables=rope, rows_per_batch=n).reshape(b, n, 3 * d)
            qkv_c = _matmul(h_c, da_w_qkv, j, mode="qkv", tables=plain).reshape(b, n_ctx, 3 * d)
            o_l = _diff_attn(qkv_l, qkv_c, da_lambda_q1[j], da_lambda_k1[j], da_lambda_q2[j],
                             da_lambda_k2[j], da_subln_g[j], _lambda_init(i))
            assert last, "context update after a differential layer is not needed at this depth"
            o_c = None
        x = _matmul(o_l.reshape(b * n, d), w_o, j, mode="resid", out_dtype=F32, resid=x.reshape(b * n, d),
                    mod=mod, gate_chunk=2, rows_per_batch=n).reshape(b, n, d)
        router_t = moe_w_router[i].T
        h2, aff = _prep(x, norm2_g[i], mod, shift_chunk=3, scale_chunk=4, router_t=router_t,
                        out_dtype=jnp.uint32)
        x = _ec_moe(x, h2, aff, mod, moe_w_gate, moe_w_up, moe_w_down, i)
        if not last:
            ctx = _matmul(o_c.reshape(b * n_ctx, d), w_o, j, mode="resid", out_dtype=F32,
                          resid=ctx.reshape(b * n_ctx, d), mod=mod, gate_chunk=2,
                          rows_per_batch=n_ctx, mod_row=b).reshape(b, n_ctx, d)
            h2c, affc = _prep(ctx, norm2_g[i], mod, shift_chunk=3, scale_chunk=4, mod_row=b,
                              router_t=router_t, out_dtype=jnp.uint32)
            ctx = _ec_moe(ctx, h2c, affc, mod, moe_w_gate, moe_w_up, moe_w_down, i, mod_row=b)
    return _prep(x, final_g, None, out_dtype=F32)
```
